```python
import math
import jax, jax.numpy as jnp
from jax import lax
import numpy as np

D_MODEL = 1024
BATCH = 8
SEQ = 4096
DEPTH = 2

GRID_W = 64
CTX_LEN = 256
CONV_DIM = D_MODEL // 2
CONV_K = 31
D_INNER = D_MODEL
SSM_HEAD_DIM = 64
N_SSM_HEADS = D_INNER // SSM_HEAD_DIM
SSM_GROUPS = 4
D_STATE = 128
SSM_CONV_K = 5
CHUNK = 128
NORM_GROUP = D_INNER // SSM_GROUPS
D_XBC = D_INNER + 2 * SSM_GROUPS * D_STATE
D_MIX = CONV_DIM + D_INNER
D_PROJ = 2 * CONV_DIM + D_INNER + D_XBC + N_SSM_HEADS
D_FF = 4 * D_MODEL
DEEPNORM_ALPHA = (2 * DEPTH) ** 0.25
DEEPNORM_BETA = (8 * DEPTH) ** -0.25
LN_EPS = 1e-5
RMS_EPS = 1e-5

kernel_name = "hybrid_conformer_ssd_diffusion_block"

F32 = jnp.float32


def layer_norm(x, g, b):
    xf = x.astype(F32)
    mu = jnp.mean(xf, axis=-1, keepdims=True)
    var = jnp.mean(jnp.square(xf - mu), axis=-1, keepdims=True)
    return ((xf - mu) * lax.rsqrt(var + LN_EPS)).astype(x.dtype) * g + b


def modulate(h, shift, scale):
    return h * (1 + scale) + shift


def dwconv_centred(u, w, b):
    pad = w.shape[0] // 2
    out = lax.conv_general_dilated(u, w[:, None, :].astype(u.dtype), window_strides=(1,),
                                   padding=[(pad, pad)], dimension_numbers=('NWC', 'WIO', 'NWC'),
                                   feature_group_count=u.shape[-1])
    return out + b


def conformer_conv(u, conv_w, conv_b, ln_g, ln_b, on_grid):
    a, g = jnp.split(u, 2, axis=-1)
    v = a * jax.nn.sigmoid(g)
    if on_grid:
        bsz, length, ch = v.shape
        rows = length // GRID_W
        v = dwconv_centred(v.reshape(bsz * rows, GRID_W, ch), conv_w, conv_b).reshape(bsz, length, ch)
    else:
        v = dwconv_centred(v, conv_w, conv_b)
    return jax.nn.silu(layer_norm(v, ln_g, ln_b))


def ssd_chunked(xs, dA, Bm, Cm, h0, with_output):
    b, l, H, P = xs.shape
    G, R, N = SSM_GROUPS, H // SSM_GROUPS, D_STATE
    nc = l // CHUNK
    xs = xs.reshape(b, nc, CHUNK, G, R, P)
    dA = dA.reshape(b, nc, CHUNK, G, R)
    Bm = Bm.reshape(b, nc, CHUNK, G, N)
    Cm = Cm.reshape(b, nc, CHUNK, G, N)
    cs = jnp.cumsum(dA, axis=2)
    decay_to_end = jnp.exp(cs[:, :, -1:] - cs)
    states = jnp.einsum('bclgn,bclgr,bclgrp->bcgrpn', Bm, decay_to_end, xs)
    chunk_decay = jnp.exp(cs[:, :, -1])

    def step(h, inp):
        st, dec = inp
        return h * dec[..., None, None] + st, h

    h_final, h_prev = lax.scan(step, h0, (jnp.moveaxis(states, 1, 0), jnp.moveaxis(chunk_decay, 1, 0)))
    if not with_output:
        return None, h_final
    h_prev = jnp.moveaxis(h_prev, 0, 1)
    cs_t = jnp.moveaxis(cs, 2, -1)
    causal = jnp.tril(jnp.ones((CHUNK, CHUNK), dtype=bool))
    decay_mat = jnp.exp(jnp.where(causal, cs_t[..., :, None] - cs_t[..., None, :], -jnp.inf))
    cb = jnp.einsum('bclgn,bcsgn->bcgls', Cm, Bm)
    y_diag = jnp.einsum('bcgls,bcgrls,bcsgrp->bclgrp', cb, decay_mat, xs)
    y_off = jnp.einsum('bclgn,bcgrpn,bclgr->bclgrp', Cm, h_prev, jnp.exp(cs))
    return (y_diag + y_off).reshape(b, l, H, P), h_final


def ssd_direction(xs, Bm, Cm, dt_raw, dt_bias_d, a_log_d, d_d, h0, with_output):
    dt = jax.nn.softplus(dt_raw + dt_bias_d.astype(F32))
    dA = dt * (-jnp.exp(a_log_d.astype(F32)))
    y, h = ssd_chunked(xs * dt[..., None], dA, Bm, Cm, h0, with_output)
    if with_output:
        y = y + d_d.astype(F32)[:, None] * xs
    return y, h


def ssd_bidirectional(xbc_l, dt_l, xbc_c, dt_c, ssm_conv_w, ssm_conv_b, dt_bias, a_log, d_skip, ctx_out):
    def prep(xbc, dt_raw):
        v = jax.nn.silu(dwconv_centred(xbc, ssm_conv_w, ssm_conv_b)).astype(F32)
        xs, Bm, Cm = jnp.split(v, [D_INNER, D_INNER + SSM_GROUPS * D_STATE], axis=-1)
        b, l = xs.shape[:2]
        return (xs.reshape(b, l, N_SSM_HEADS, SSM_HEAD_DIM), Bm.reshape(b, l, SSM_GROUPS, D_STATE),
                Cm.reshape(b, l, SSM_GROUPS, D_STATE), dt_raw.astype(F32))

    lat = prep(xbc_l, dt_l)
    ctx = prep(xbc_c, dt_c)
    b = xbc_c.shape[0]
    h_zero = jnp.zeros((b, SSM_GROUPS, N_SSM_HEADS // SSM_GROUPS, SSM_HEAD_DIM, D_STATE), F32)
    ys_l, ys_c = [], []
    for d in range(2):
        flip = (lambda a: jnp.flip(a, axis=1)) if d == 1 else (lambda a: a)
        y_cd, h_c = ssd_direction(*[flip(a) for a in ctx], dt_bias[d], a_log[d], d_skip[d], h_zero, ctx_out)
        y_ld, _ = ssd_direction(*[flip(a) for a in lat], dt_bias[d], a_log[d], d_skip[d], h_c, True)
        ys_l.append(flip(y_ld))
        if ctx_out:
            ys_c.append(flip(y_cd))
    bl, ll = xbc_l.shape[:2]
    y_lat = (ys_l[0] + ys_l[1]).reshape(bl, ll, D_INNER)
    y_ctx = (ys_c[0] + ys_c[1]).reshape(b, xbc_c.shape[1], D_INNER) if ctx_out else None
    return y_lat, y_ctx


def gated_rmsnorm(y, z, w):
    g = y * jax.nn.silu(z.astype(F32))
    b, l, _ = g.shape
    g = g.reshape(b, l, SSM_GROUPS, NORM_GROUP)
    g = g * lax.rsqrt(jnp.mean(jnp.square(g), axis=-1, keepdims=True) + RMS_EPS)
    return g.reshape(b, l, D_INNER).astype(z.dtype) * w


def mixer(h_lat, h_ctx, w_in, conv_w, conv_b, conv_ln_g, conv_ln_b, ssm_conv_w, ssm_conv_b,
          dt_bias, a_log, d_skip, ssm_norm_w, w_out, ctx_out):
    cuts = [2 * CONV_DIM, 2 * CONV_DIM + D_INNER, 2 * CONV_DIM + D_INNER + D_XBC]
    u_l, z_l, xbc_l, dt_l = jnp.split(h_lat @ w_in, cuts, axis=-1)
    u_c, z_c, xbc_c, dt_c = jnp.split(h_ctx @ w_in, cuts, axis=-1)
    conv_l = conformer_conv(u_l, conv_w, conv_b, conv_ln_g, conv_ln_b, True)
    y_l, y_c = ssd_bidirectional(xbc_l, dt_l, xbc_c, dt_c, ssm_conv_w, ssm_conv_b, dt_bias, a_log, d_skip, ctx_out)
    out_l = jnp.concatenate([conv_l, gated_rmsnorm(y_l, z_l, ssm_norm_w)], axis=-1) @ w_out
    if not ctx_out:
        return out_l, None
    conv_c = conformer_conv(u_c, conv_w, conv_b, conv_ln_g, conv_ln_b, False)
    out_c = jnp.concatenate([conv_c, gated_rmsnorm(y_c, z_c, ssm_norm_w)], axis=-1) @ w_out
    return out_l, out_c


def sq_relu_mlp(h, w1, w2):
    return jnp.square(jax.nn.relu(h @ w1)) @ w2


def setup_inputs(seed: int = 0) -> dict:
    key = jax.random.key(seed)
    ks = jax.random.split(key, 26)

    def nrm(k, shape, s):
        return s * jax.random.normal(k, shape, F32)

    H = N_SSM_HEADS
    dt0 = jnp.exp(jax.random.uniform(ks[13], (DEPTH, 2, H), F32, math.log(1e-3), math.log(1e-1)))
    return {
        'x': nrm(ks[0], (BATCH, SEQ, D_MODEL), 1.0),
        'c': nrm(ks[1], (BATCH, D_MODEL), 1.0),
        'ctx': nrm(ks[2], (BATCH, CTX_LEN, D_MODEL), 1.0),
        'c_ctx': nrm(ks[3], (D_MODEL,), 1.0),
        'w_mod': nrm(ks[4], (DEPTH, D_MODEL, 6 * D_MODEL), D_MODEL ** -0.5),
        'b_mod': nrm(ks[5], (DEPTH, 6 * D_MODEL), 0.01),
        'w_in': nrm(ks[6], (DEPTH, D_MODEL, D_PROJ), D_MODEL ** -0.5),
        'conv_w': nrm(ks[7], (DEPTH, CONV_K, CONV_DIM), CONV_K ** -0.5),
        'conv_b': nrm(ks[8], (DEPTH, CONV_DIM), 0.01),
        'conv_ln_g': 1.0 + nrm(ks[9], (DEPTH, CONV_DIM), 0.01),
        'conv_ln_b': nrm(ks[10], (DEPTH, CONV_DIM), 0.01),
        'ssm_conv_w': nrm(ks[11], (DEPTH, SSM_CONV_K, D_XBC), SSM_CONV_K ** -0.5),
        'ssm_conv_b': nrm(ks[12], (DEPTH, D_XBC), 0.01),
        'dt_bias': dt0 + jnp.log(-jnp.expm1(-dt0)),
        'a_log': jnp.log(jax.random.uniform(ks[14], (DEPTH, 2, H), F32, 1.0, 16.0)),
        'd_skip': 1.0 + nrm(ks[15], (DEPTH, 2, H), 0.01),
        'ssm_norm_w': 1.0 + nrm(ks[16], (DEPTH, D_INNER), 0.01),
        'w_out': nrm(ks[17], (DEPTH, D_MIX, D_MODEL), DEEPNORM_BETA * D_MIX ** -0.5),
        'ln1_g': 1.0 + nrm(ks[18], (DEPTH, D_MODEL), 0.01),
        'ln1_b': nrm(ks[19], (DEPTH, D_MODEL), 0.01),
        'w1': nrm(ks[20], (DEPTH, D_MODEL, D_FF), D_MODEL ** -0.5),
        'w2': nrm(ks[21], (DEPTH, D_FF, D_MODEL), DEEPNORM_BETA * D_FF ** -0.5),
        'ln2_g': 1.0 + nrm(ks[22], (DEPTH, D_MODEL), 0.01),
        'ln2_b': nrm(ks[23], (DEPTH, D_MODEL), 0.01),
    }


def reference(x, c, ctx, c_ctx, w_mod, b_mod, w_in, conv_w, conv_b, conv_ln_g, conv_ln_b,
              ssm_conv_w, ssm_conv_b, dt_bias, a_log, d_skip, ssm_norm_w, w_out,
              ln1_g, ln1_b, w1, w2, ln2_g, ln2_b):
    x_l, x_c = x, ctx
    silu_c = jax.nn.silu(c)
    silu_cc = jax.nn.silu(c_ctx)
    for i in range(DEPTH):
        last = i == DEPTH - 1
        sh1_l, sc1_l, g1_l, sh2_l, sc2_l, g2_l = jnp.split((silu_c @ w_mod[i] + b_mod[i])[:, None, :], 6, axis=-1)
        sh1_c, sc1_c, g1_c, sh2_c, sc2_c, g2_c = jnp.split((silu_cc @ w_mod[i] + b_mod[i])[None, None, :], 6, axis=-1)
        mix_l, mix_c = mixer(modulate(x_l, sh1_l, sc1_l), modulate(x_c, sh1_c, sc1_c), w_in[i],
                             conv_w[i], conv_b[i], conv_ln_g[i], conv_ln_b[i], ssm_conv_w[i], ssm_conv_b[i],
                             dt_bias[i], a_log[i], d_skip[i], ssm_norm_w[i], w_out[i], not last)
        x_l = layer_norm(DEEPNORM_ALPHA * x_l + g1_l * mix_l, ln1_g[i], ln1_b[i])
        x_l = layer_norm(DEEPNORM_ALPHA * x_l + g2_l * sq_relu_mlp(modulate(x_l, sh2_l, sc2_l), w1[i], w2[i]),
                         ln2_g[i], ln2_b[i])
        if not last:
            x_c = layer_norm(DEEPNORM_ALPHA * x_c + g1_c * mix_c, ln1_g[i], ln1_b[i])
            x_c = layer_norm(DEEPNORM_ALPHA * x_c + g2_c * sq_relu_mlp(modulate(x_c, sh2_c, sc2_c), w1[i], w2[i]),
                             ln2_g[i], ln2_b[i])
    return x_l
```

```python
import functools

import numpy as np
import jax
import jax.numpy as jnp
from jax import lax
from jax.experimental import pallas as pl
from jax.experimental.pallas import tpu as pltpu

F32 = jnp.float32
BF16 = jnp.bfloat16

GRID_W = 64
CONV_K = 31
CONV_PAD = CONV_K // 2
HEAD_DIM = 64
N_HEADS = 16
SSM_GROUPS = 4
HEADS_PER_GROUP = N_HEADS // SSM_GROUPS
D_STATE = 128
SSM_CONV_K = 5
SSM_PAD = SSM_CONV_K // 2
CHUNK = 128
LN_EPS = 1e-5
RMS_EPS = 1e-5

LANES = 128
SUBLANES = 8
VMEM_LIMIT = 56 * 1024 * 1024

N_TILE = 512
CONV_ROWS = 64
DT_COPIES = LANES // N_HEADS


def _token_tile(t):
    return 512 if t % 512 == 0 else 256


def _params(sem):
    return pltpu.CompilerParams(dimension_semantics=sem, vmem_limit_bytes=VMEM_LIMIT)


def _dot(a, b):
    return jnp.dot(a, b, preferred_element_type=F32)


def _sigmoid(x):
    return 1.0 / (1.0 + jnp.exp(-x))


def _softplus(x):
    return jnp.maximum(x, 0.0) + jnp.log1p(jnp.exp(-jnp.abs(x)))


def _layer_norm(u, g, b):
    mu = jnp.mean(u, axis=-1, keepdims=True)
    d = u - mu
    var = jnp.mean(d * d, axis=-1, keepdims=True)
    return d * lax.rsqrt(var + LN_EPS) * g + b


def _full(shape):
    nd = len(shape)
    return pl.BlockSpec(shape, lambda *_: (0,) * nd)


def _mod_kernel(c_ref, w_ref, b_ref, o_ref):
    c = c_ref[...]
    a = c * _sigmoid(c)
    a_hi = a.astype(BF16)
    a_lo = (a - a_hi.astype(F32)).astype(BF16)
    w = w_ref[...]
    w_hi = w.astype(BF16)
    w_lo = (w - w_hi.astype(F32)).astype(BF16)
    o_ref[...] = _dot(a_hi, w_hi) + _dot(a_lo, w_hi) + _dot(a_hi, w_lo) + b_ref[...]


def _mod_call(c_all, w_mod, b_mod):
    rows, d = c_all.shape
    n = w_mod.shape[1]
    tn = 1536
    return pl.pallas_call(
        _mod_kernel,
        grid=(n // tn,),
        in_specs=[_full((rows, d)),
                  pl.BlockSpec((d, tn), lambda j: (0, j)),
                  pl.BlockSpec((1, tn), lambda j: (0, j))],
        out_specs=pl.BlockSpec((rows, tn), lambda j: (0, j)),
        out_shape=jax.ShapeDtypeStruct((rows, n), F32),
        compiler_params=_params(("parallel",)),
        name="mod",
    )(c_all, w_mod, b_mod.reshape(1, n))


def _inproj_kernel(x_ref, sh_ref, sc_ref, w_ref, wdt_ref, v_ref, z_ref, xbc_ref, dt_ref, *,
                   conv_dim, d_inner, d_xbc):
    hb = (x_ref[0] * (1.0 + sc_ref[0]) + sh_ref[0]).astype(BF16)
    a = _dot(hb, w_ref[:, 0:conv_dim])
    g = _dot(hb, w_ref[:, conv_dim:2 * conv_dim])
    v_ref[0] = a * _sigmoid(g)
    off = 2 * conv_dim
    for j in range(d_inner // N_TILE):
        z_ref[0, :, j * N_TILE:(j + 1) * N_TILE] = _dot(hb, w_ref[:, off + j * N_TILE:off + (j + 1) * N_TILE])
    off += d_inner
    for j in range(d_xbc // N_TILE):
        xbc_ref[0, :, j * N_TILE:(j + 1) * N_TILE] = _dot(hb, w_ref[:, off + j * N_TILE:off + (j + 1) * N_TILE])
    dt_ref[0] = _dot(hb, wdt_ref[...])


def _inproj_call(x, sh, sc, w_main, w_dt, conv_dim, d_inner, d_xbc):
    b, t, d = x.shape
    tm = _token_tile(t)
    tok = lambda width: pl.BlockSpec((1, tm, width), lambda i, j: (i, j, 0))
    vec = pl.BlockSpec((1, 1, d), lambda i, j: (i, 0, 0))
    kern = functools.partial(_inproj_kernel, conv_dim=conv_dim, d_inner=d_inner, d_xbc=d_xbc)
    return pl.pallas_call(
        kern,
        grid=(b, t // tm),
        in_specs=[tok(d), vec, vec, _full(w_main.shape), _full(w_dt.shape)],
        out_specs=[tok(conv_dim), tok(d_inner), tok(d_xbc), tok(LANES)],
        out_shape=[jax.ShapeDtypeStruct((b, t, conv_dim), F32),
                   jax.ShapeDtypeStruct((b, t, d_inner), F32),
                   jax.ShapeDtypeStruct((b, t, d_xbc), F32),
                   jax.ShapeDtypeStruct((b, t, LANES), F32)],
        compiler_params=_params(("parallel", "parallel")),
        name="inproj",
    )(x, sh, sc, w_main, w_dt)


def _cconv_kernel(v_ref, w_ref, b_ref, g_ref, beta_ref, o_ref, pad_ref, acc_ref, *, seg, nseg):
    ch = v_ref.shape[-1]
    halo = 2 * SUBLANES
    pad_ref[0:halo, :] = jnp.zeros((halo, ch), F32)
    pad_ref[halo + seg:2 * halo + seg, :] = jnp.zeros((halo, ch), F32)

    def body(r, carry):
        start = pl.multiple_of(r * seg, seg)
        pad_ref[halo:halo + seg, :] = v_ref[0, pl.ds(start, seg), :]
        for rb in range(seg // CONV_ROWS):
            for cb in range(ch // LANES):
                lanes = slice(cb * LANES, (cb + 1) * LANES)
                acc = jnp.zeros((CONV_ROWS, LANES), F32) + b_ref[:, lanes]
                for k in range(CONV_K):
                    r0 = rb * CONV_ROWS + halo - CONV_PAD + k
                    acc = acc + w_ref[k:k + 1, lanes] * pad_ref[r0:r0 + CONV_ROWS, lanes]
                acc_ref[rb * CONV_ROWS:(rb + 1) * CONV_ROWS, lanes] = acc
        y = _layer_norm(acc_ref[...], g_ref[...], beta_ref[...])
        o_ref[0, pl.ds(start, seg), :] = (y * _sigmoid(y)).astype(BF16)
        return carry

    lax.fori_loop(0, nseg, body, 0)


def _cconv_call(v, conv_w, conv_b, ln_g, ln_b, seg):
    b, t, ch = v.shape
    tm = _token_tile(t)
    nseg = tm // seg
    tok = pl.BlockSpec((1, tm, ch), lambda i, j: (i, j, 0))
    kern = functools.partial(_cconv_kernel, seg=seg, nseg=nseg)
    return pl.pallas_call(
        kern,
        grid=(b, t // tm),
        in_specs=[tok, _full((CONV_K, ch)), _full((1, ch)), _full((1, ch)), _full((1, ch))],
        out_specs=tok,
        out_shape=jax.ShapeDtypeStruct((b, t, ch), BF16),
        scratch_shapes=[pltpu.VMEM((seg + 4 * SUBLANES, ch), F32), pltpu.VMEM((seg, ch), F32)],
        compiler_params=_params(("parallel", "parallel")),
        name="cconv",
    )(v, conv_w, conv_b.reshape(1, ch), ln_g.reshape(1, ch), ln_b.reshape(1, ch))


def _cumsum_rows(x, reverse):
    n = x.shape[0]
    row = lax.broadcasted_iota(jnp.int32, x.shape, 0)
    k = 1
    while k < n:
        if reverse:
            x = x + jnp.where(row < n - k, pltpu.roll(x, n - k, 0), 0.0)
        else:
            x = x + jnp.where(row >= k, pltpu.roll(x, k, 0), 0.0)
        k *= 2
    return x


def _expand_heads(w, e, eexp_ref):
    lane = lax.broadcasted_iota(jnp.int32, w.shape, 1)
    x = jnp.where(lane < 2 * N_HEADS, w, e)
    x_hi = x.astype(BF16).astype(F32)
    x_lo = x - x_hi
    use_hi = ((lane // N_HEADS) % 2) == 0
    lhs = jnp.where(lane < 4 * N_HEADS, jnp.where(use_hi, x_hi, x_lo), 0.0).astype(BF16)
    return _dot(lhs, eexp_ref[...])


def _expand_matrix():
    d_inner = N_HEADS * HEAD_DIM
    m = np.zeros((LANES, 2 * d_inner), np.float32)
    for j in range(4 * N_HEADS):
        h = j % N_HEADS
        base = (j // (2 * N_HEADS)) * d_inner
        m[j, base + h * HEAD_DIM:base + (h + 1) * HEAD_DIM] = 1.0
    return jnp.asarray(m, BF16)


def _state_step(h_ref, c_g, bt_g, xs_g, exp_w, exp_e, decay_row, gs):
    h_g = h_ref[:, gs]
    y_off = _dot(c_g, h_g.astype(BF16)) * exp_e[:, gs]
    xd = (xs_g * exp_w[:, gs]).astype(BF16)
    h_ref[:, gs] = h_g * decay_row[:, gs] + _dot(bt_g, xd)
    return y_off


def _ssd_a_kernel(x_ref, xp_ref, xn_ref, dt_ref, cw_ref, cb_ref, bias_f_ref, bias_b_ref,
                  alog_f_ref, alog_b_ref, dsk_f_ref, dsk_b_ref, eexp_ref, h0_ref,
                  y_ref, xact_ref, hfin_ref, pad_ref, act_ref, h_ref):
    c = pl.program_id(1)
    nc = pl.num_programs(1)
    d_inner = N_HEADS * HEAD_DIM
    gw = HEADS_PER_GROUP * HEAD_DIM

    @pl.when(c == 0)
    def _():
        h_ref[...] = h0_ref[0]

    pad_ref[0:SUBLANES, :] = jnp.where(c > 0, xp_ref[0], 0.0)
    pad_ref[SUBLANES:SUBLANES + CHUNK, :] = x_ref[0]
    pad_ref[SUBLANES + CHUNK:2 * SUBLANES + CHUNK, :] = jnp.where(c < nc - 1, xn_ref[0], 0.0)
    for cbk in range(x_ref.shape[-1] // LANES):
        lanes = slice(cbk * LANES, (cbk + 1) * LANES)
        acc = jnp.zeros((CHUNK, LANES), F32) + cb_ref[:, lanes]
        for k in range(SSM_CONV_K):
            r0 = SUBLANES - SSM_PAD + k
            acc = acc + cw_ref[k:k + 1, lanes] * pad_ref[r0:r0 + CHUNK, lanes]
        act = acc * _sigmoid(acc)
        act_ref[:, lanes] = act
        xact_ref[0, :, lanes] = act.astype(BF16)

    raw = dt_ref[0]
    dt_f = _softplus(raw + bias_f_ref[...])
    dt_b = _softplus(raw + bias_b_ref[...])
    cs_f = _cumsum_rows(dt_f * (-jnp.exp(alog_f_ref[...])), reverse=False)
    cs_b = _cumsum_rows(dt_b * (-jnp.exp(alog_b_ref[...])), reverse=True)
    lane = lax.broadcasted_iota(jnp.int32, raw.shape, 1)
    blk = lane // N_HEADS
    stats = jnp.where(blk == 0, cs_f,
                      jnp.where(blk == 1, cs_b,
                                jnp.where(blk == 2, cs_f - jnp.log(dt_f),
                                          jnp.where(blk == 3, cs_b - jnp.log(dt_b),
                                                    jnp.log(dt_f + dt_b)))))
    stats_t = stats.T
    tot_f = cs_f[CHUNK - 1:CHUNK, :]
    exp_all = _expand_heads(dt_f * jnp.exp(tot_f - cs_f), jnp.exp(cs_f), eexp_ref)
    exp_w = exp_all[:, :d_inner]
    exp_e = exp_all[:, d_inner:]
    decay_row = exp_e[CHUNK - 1:CHUNK, :]

    row_i = lax.broadcasted_iota(jnp.int32, (CHUNK, CHUNK), 0)
    col_i = lax.broadcasted_iota(jnp.int32, (CHUNK, CHUNK), 1)
    src_before = col_i < row_i
    src_after = col_i > row_i
    left_half = col_i < HEAD_DIM
    dsk = dsk_f_ref[...] + dsk_b_ref[...]

    for g in range(SSM_GROUPS):
        gs = slice(g * gw, (g + 1) * gw)
        b_g = act_ref[:, d_inner + g * D_STATE:d_inner + (g + 1) * D_STATE]
        c_g = act_ref[:, d_inner + (SSM_GROUPS + g) * D_STATE:d_inner + (SSM_GROUPS + g + 1) * D_STATE]
        b_gb = b_g.astype(BF16)
        c_gb = c_g.astype(BF16)
        cbm = lax.dot_general(c_gb, b_gb, (((1,), (1,)), ((), ())), preferred_element_type=F32)
        xs_g = act_ref[:, gs]
        y_off = _state_step(h_ref, c_gb, b_g.T.astype(BF16), xs_g, exp_w, exp_e, decay_row, gs)
        y_pairs = []
        for pr in range(HEADS_PER_GROUP // 2):
            mats = []
            for r in range(2):
                h = g * HEADS_PER_GROUP + 2 * pr + r
                arg = jnp.where(
                    src_before, stats[:, h:h + 1] - stats_t[2 * N_HEADS + h:2 * N_HEADS + h + 1, :],
                    jnp.where(src_after,
                              stats[:, N_HEADS + h:N_HEADS + h + 1]
                              - stats_t[3 * N_HEADS + h:3 * N_HEADS + h + 1, :],
                              stats_t[4 * N_HEADS + h:4 * N_HEADS + h + 1, :]))
                mats.append((cbm * jnp.exp(arg)).astype(BF16))
            x_pair = xs_g[:, pr * LANES:(pr + 1) * LANES]
            rhs = jnp.concatenate([jnp.where(left_half, x_pair, 0.0).astype(BF16),
                                   jnp.where(left_half, 0.0, x_pair).astype(BF16)], axis=0)
            y_pairs.append(_dot(jnp.concatenate(mats, axis=1), rhs))
        y_ref[0, :, gs] = jnp.concatenate(y_pairs, axis=1) + y_off + dsk[:, gs] * xs_g

    @pl.when(c == nc - 1)
    def _():
        hfin_ref[0] = h_ref[...]


def _ssd_a_call(xbc, dt8, cw, cb, bias_f, bias_b, alog_f, alog_b, dsk_f, dsk_b, eexp, h0):
    b, t, dx = xbc.shape
    nc = t // CHUNK
    d_inner = N_HEADS * HEAD_DIM
    rows8 = CHUNK // SUBLANES
    last8 = t // SUBLANES - 1
    chunk = lambda width: pl.BlockSpec((1, CHUNK, width), lambda i, c: (i, c, 0))
    state = pl.BlockSpec((1, D_STATE, d_inner), lambda i, c: (i, 0, 0))
    return pl.pallas_call(
        _ssd_a_kernel,
        grid=(b, nc),
        in_specs=[chunk(dx),
                  pl.BlockSpec((1, SUBLANES, dx), lambda i, c: (i, jnp.maximum(c * rows8 - 1, 0), 0)),
                  pl.BlockSpec((1, SUBLANES, dx), lambda i, c: (i, jnp.minimum((c + 1) * rows8, last8), 0)),
                  chunk(LANES),
                  _full(cw.shape), _full(cb.shape), _full(bias_f.shape), _full(bias_b.shape),
                  _full(alog_f.shape), _full(alog_b.shape), _full(dsk_f.shape), _full(dsk_b.shape),
                  _full(eexp.shape), state],
        out_specs=[chunk(d_inner), chunk(dx), state],
        out_shape=[jax.ShapeDtypeStruct((b, t, d_inner), F32),
                   jax.ShapeDtypeStruct((b, t, dx), BF16),
                   jax.ShapeDtypeStruct((b, D_STATE, d_inner), F32)],
        scratch_shapes=[pltpu.VMEM((CHUNK + 2 * SUBLANES, dx), F32),
                        pltpu.VMEM((CHUNK, dx), F32),
                        pltpu.VMEM((D_STATE, d_inner), F32)],
        compiler_params=_params(("arbitrary", "arbitrary")),
        name="ssd_a",
    )(xbc, xbc, xbc, dt8, cw, cb, bias_f, bias_b, alog_f, alog_b, dsk_f, dsk_b, eexp, h0)


def _ssd_b_kernel(xact_ref, dt_ref, y_ref, z_ref, bias_b_ref, alog_b_ref, eexp_ref, nw_ref, h0_ref,
                  o_ref, hfin_ref, h_ref):
    c = pl.program_id(1)
    nc = pl.num_programs(1)
    d_inner = N_HEADS * HEAD_DIM
    gw = HEADS_PER_GROUP * HEAD_DIM

    @pl.when(c == 0)
    def _():
        h_ref[...] = h0_ref[0]

    dt_b = _softplus(dt_ref[0] + bias_b_ref[...])
    cs_b = _cumsum_rows(dt_b * (-jnp.exp(alog_b_ref[...])), reverse=True)
    tot_b = cs_b[0:1, :]
    exp_all = _expand_heads(dt_b * jnp.exp(tot_b - cs_b), jnp.exp(cs_b), eexp_ref)
    exp_w = exp_all[:, :d_inner]
    exp_e = exp_all[:, d_inner:]
    decay_row = exp_e[0:1, :]

    for g in range(SSM_GROUPS):
        gs = slice(g * gw, (g + 1) * gw)
        b_g = xact_ref[0, :, d_inner + g * D_STATE:d_inner + (g + 1) * D_STATE]
        c_g = xact_ref[0, :, d_inner + (SSM_GROUPS + g) * D_STATE:d_inner + (SSM_GROUPS + g + 1) * D_STATE]
        xs_g = xact_ref[0, :, gs].astype(F32)
        bt_g = b_g.astype(F32).T.astype(BF16)
        y = y_ref[0, :, gs] + _state_step(h_ref, c_g, bt_g, xs_g, exp_w, exp_e, decay_row, gs)
        z = z_ref[0, :, gs]
        gated = y * (z * _sigmoid(z))
        ms = jnp.mean(gated * gated, axis=-1, keepdims=True)
        o_ref[0, :, gs] = (gated * lax.rsqrt(ms + RMS_EPS) * nw_ref[:, gs]).astype(BF16)

    @pl.when(c == nc - 1)
    def _():
        hfin_ref[0] = h_ref[...]


def _ssd_b_call(xact, dt8, ypart, z, bias_b, alog_b, eexp, norm_w, h0):
    b, t, dx = xact.shape
    nc = t // CHUNK
    d_inner = N_HEADS * HEAD_DIM
    chunk = lambda width: pl.BlockSpec((1, CHUNK, width), lambda i, c: (i, nc - 1 - c, 0))
    state = pl.BlockSpec((1, D_STATE, d_inner), lambda i, c: (i, 0, 0))
    return pl.pallas_call(
        _ssd_b_kernel,
        grid=(b, nc),
        in_specs=[chunk(dx), chunk(LANES), chunk(d_inner), chunk(d_inner),
                  _full(bias_b.shape), _full(alog_b.shape), _full(eexp.shape), _full(norm_w.shape), state],
        out_specs=[chunk(d_inner), state],
        out_shape=[jax.ShapeDtypeStruct((b, t, d_inner), BF16),
                   jax.ShapeDtypeStruct((b, D_STATE, d_inner), F32)],
        scratch_shapes=[pltpu.VMEM((D_STATE, d_inner), F32)],
        compiler_params=_params(("arbitrary", "arbitrary")),
        name="ssd_b",
    )(xact, dt8, ypart, z, bias_b, alog_b, eexp, norm_w, h0)


def _outproj_kernel(cv_ref, g_ref, x_ref, gate_ref, wc_ref, ws_ref, lg_ref, lb_ref, o_ref, *, alpha):
    mix = _dot(cv_ref[0], wc_ref[...]) + _dot(g_ref[0], ws_ref[...])
    u = alpha * x_ref[0] + gate_ref[0] * mix
    o_ref[0] = _layer_norm(u, lg_ref[...], lb_ref[...])


def _outproj_call(cv, gs, x, gate, w_conv, w_ssd, ln_g, ln_b, alpha):
    b, t, d = x.shape
    tm = _token_tile(t)
    tok = lambda width: pl.BlockSpec((1, tm, width), lambda i, j: (i, j, 0))
    vec = pl.BlockSpec((1, 1, d), lambda i, j: (i, 0, 0))
    return pl.pallas_call(
        functools.partial(_outproj_kernel, alpha=alpha),
        grid=(b, t // tm),
        in_specs=[tok(cv.shape[-1]), tok(gs.shape[-1]), tok(d), vec,
                  _full(w_conv.shape), _full(w_ssd.shape), _full((1, d)), _full((1, d))],
        out_specs=tok(d),
        out_shape=jax.ShapeDtypeStruct((b, t, d), F32),
        compiler_params=_params(("parallel", "parallel")),
        name="outproj",
    )(cv, gs, x, gate, w_conv, w_ssd, ln_g.reshape(1, d), ln_b.reshape(1, d))


def _mlp_kernel(x_ref, sh_ref, sc_ref, gate_ref, w1_ref, w2_ref, lg_ref, lb_ref, o_ref, *, alpha):
    x = x_ref[0]
    hb = (x * (1.0 + sc_ref[0]) + sh_ref[0]).astype(BF16)
    acc = jnp.zeros(x.shape, F32)
    for j in range(w1_ref.shape[1] // N_TILE):
        cols = slice(j * N_TILE, (j + 1) * N_TILE)
        t = jnp.maximum(_dot(hb, w1_ref[:, cols]), 0.0)
        acc = acc + _dot((t * t).astype(BF16), w2_ref[cols, :])
    u = alpha * x + gate_ref[0] * acc
    o_ref[0] = _layer_norm(u, lg_ref[...], lb_ref[...])


def _mlp_call(x, sh, sc, gate, w1, w2, ln_g, ln_b, alpha):
    b, t, d = x.shape
    tm = _token_tile(t)
    tok = pl.BlockSpec((1, tm, d), lambda i, j: (i, j, 0))
    vec = pl.BlockSpec((1, 1, d), lambda i, j: (i, 0, 0))
    return pl.pallas_call(
        functools.partial(_mlp_kernel, alpha=alpha),
        grid=(b, t // tm),
        in_specs=[tok, vec, vec, vec, _full(w1.shape), _full(w2.shape), _full((1, d)), _full((1, d))],
        out_specs=tok,
        out_shape=jax.ShapeDtypeStruct((b, t, d), F32),
        compiler_params=_params(("parallel", "parallel")),
        name="mlp",
    )(x, sh, sc, gate, w1, w2, ln_g.reshape(1, d), ln_b.reshape(1, d))


def _mixer(x, mods, lw, h0_f, h0_b, seg):
    sh1, sc1 = mods[0], mods[1]
    v, z, xbc, dt8 = _inproj_call(x, sh1, sc1, lw["w_main"], lw["w_dt"],
                                  lw["conv_dim"], lw["d_inner"], lw["d_xbc"])
    ypart, xact, hfin_f = _ssd_a_call(xbc, dt8, lw["ssm_cw"], lw["ssm_cb"], lw["bias_f"], lw["bias_b"],
                                      lw["alog_f"], lw["alog_b"], lw["dsk_f"], lw["dsk_b"], lw["eexp"], h0_f)
    gated, hfin_b = _ssd_b_call(xact, dt8, ypart, z, lw["bias_b"], lw["alog_b"], lw["eexp"],
                                lw["norm_w"], h0_b)
    return v, gated, hfin_f, hfin_b


def _finish_layer(x, v, gated, mods, lw, seg, alpha):
    cv = _cconv_call(v, lw["conv_w"], lw["conv_b"], lw["conv_ln_g"], lw["conv_ln_b"], seg)
    x1 = _outproj_call(cv, gated, x, mods[2], lw["w_out_conv"], lw["w_out_ssd"],
                       lw["ln1_g"], lw["ln1_b"], alpha)
    return _mlp_call(x1, mods[3], mods[4], mods[5], lw["w1"], lw["w2"], lw["ln2_g"], lw["ln2_b"], alpha)


def kernel(x, c, ctx, c_ctx, w_mod, b_mod, w_in, conv_w, conv_b, conv_ln_g, conv_ln_b, ssm_conv_w,
           ssm_conv_b, dt_bias, a_log, d_skip, ssm_norm_w, w_out, ln1_g, ln1_b, w1, w2, ln2_g, ln2_b):
    depth = w_mod.shape[0]
    bsz, _, d = x.shape
    conv_dim = conv_w.shape[-1]
    d_xbc = ssm_conv_w.shape[-1]
    d_inner = ssm_norm_w.shape[-1]
    assert d_inner == N_HEADS * HEAD_DIM and d_xbc == d_inner + 2 * SSM_GROUPS * D_STATE
    assert x.shape[1] % (CHUNK * 4) == 0 and ctx.shape[1] % CHUNK == 0
    alpha = (2 * depth) ** 0.25
    n_main = 2 * conv_dim + d_inner + d_xbc

    mod_rows = 2 * SUBLANES
    c_all = jnp.zeros((mod_rows, d), F32).at[:bsz].set(c).at[bsz].set(c_ctx)
    eexp = _expand_matrix()
    tile_heads = lambda a: jnp.tile(a.astype(F32), DT_COPIES).reshape(1, LANES)
    per_lane = lambda a: jnp.repeat(a.astype(F32), HEAD_DIM).reshape(1, d_inner)

    x_l, x_c = x, ctx
    for i in range(depth):
        last = i == depth - 1
        lw = dict(
            conv_dim=conv_dim, d_inner=d_inner, d_xbc=d_xbc,
            w_main=w_in[i][:, :n_main].astype(BF16),
            w_dt=jnp.tile(w_in[i][:, n_main:], (1, DT_COPIES)).astype(BF16),
            conv_w=conv_w[i], conv_b=conv_b[i], conv_ln_g=conv_ln_g[i], conv_ln_b=conv_ln_b[i],
            ssm_cw=ssm_conv_w[i], ssm_cb=ssm_conv_b[i].reshape(1, d_xbc),
            bias_f=tile_heads(dt_bias[i, 0]), bias_b=tile_heads(dt_bias[i, 1]),
            alog_f=tile_heads(a_log[i, 0]), alog_b=tile_heads(a_log[i, 1]),
            dsk_f=per_lane(d_skip[i, 0]), dsk_b=per_lane(d_skip[i, 1]),
            eexp=eexp, norm_w=ssm_norm_w[i].reshape(1, d_inner),
            w_out_conv=w_out[i][:conv_dim].astype(BF16), w_out_ssd=w_out[i][conv_dim:].astype(BF16),
            ln1_g=ln1_g[i], ln1_b=ln1_b[i], w1=w1[i].astype(BF16), w2=w2[i].astype(BF16),
            ln2_g=ln2_g[i], ln2_b=ln2_b[i],
        )
        mod = _mod_call(c_all, w_mod[i], b_mod[i])
        mods_l = [mod[:bsz, j * d:(j + 1) * d].reshape(bsz, 1, d) for j in range(6)]
        mods_c = [jnp.broadcast_to(mod[bsz, j * d:(j + 1) * d].reshape(1, 1, d), (bsz, 1, d)) for j in range(6)]

        h_zero = jnp.zeros((bsz, D_STATE, d_inner), F32)
        v_c, gated_c, hc_f, hc_b = _mixer(x_c, mods_c, lw, h_zero, h_zero, ctx.shape[1])
        v_l, gated_l, _, _ = _mixer(x_l, mods_l, lw, hc_f, hc_b, GRID_W)
        x_l = _finish_layer(x_l, v_l, gated_l, mods_l, lw, GRID_W, alpha)
        if not last:
            x_c = _finish_layer(x_c, v_c, gated_c, mods_c, lw, ctx.shape[1], alpha)
    return x_l
```

```python
import functools

import numpy as np
import jax
import jax.numpy as jnp
from jax import lax
from jax.experimental import pallas as pl
from jax.experimental.pallas import tpu as pltpu

F32 = jnp.float32
BF16 = jnp.bfloat16

GRID_W = 64
CONV_K = 31
CONV_PAD = CONV_K // 2
HEAD_DIM = 64
N_HEADS = 16
SSM_GROUPS = 4
HEADS_PER_GROUP = N_HEADS // SSM_GROUPS
D_STATE = 128
SSM_CONV_K = 5
SSM_PAD = SSM_CONV_K // 2
CHUNK = 128
LN_EPS = 1e-5
RMS_EPS = 1e-5

LANES = 128
SUBLANES = 8
VMEM_LIMIT = 56 * 1024 * 1024

N_TILE = 512
CONV_ROWS = 64
DT_COPIES = LANES // N_HEADS


def _token_tile(t):
    return 512 if t % 512 == 0 else 256


def _params(sem):
    return pltpu.CompilerParams(dimension_semantics=sem, vmem_limit_bytes=VMEM_LIMIT)


def _dot(a, b):
    return jnp.dot(a, b, preferred_element_type=F32)


def _sigmoid(x):
    return 1.0 / (1.0 + jnp.exp(-x))


def _softplus(x):
    return jnp.maximum(x, 0.0) + jnp.log1p(jnp.exp(-jnp.abs(x)))


def _layer_norm(u, g, b):
    mu = jnp.mean(u, axis=-1, keepdims=True)
    d = u - mu
    var = jnp.mean(d * d, axis=-1, keepdims=True)
    return d * lax.rsqrt(var + LN_EPS) * g + b


def _full(shape):
    nd = len(shape)
    return pl.BlockSpec(shape, lambda *_: (0,) * nd)


def _mod_kernel(c_ref, w_ref, b_ref, o_ref):
    c = c_ref[...]
    a = c * _sigmoid(c)
    a_hi = a.astype(BF16)
    a_lo = (a - a_hi.astype(F32)).astype(BF16)
    w = w_ref[...]
    w_hi = w.astype(BF16)
    w_lo = (w - w_hi.astype(F32)).astype(BF16)
    o_ref[...] = _dot(a_hi, w_hi) + _dot(a_lo, w_hi) + _dot(a_hi, w_lo) + b_ref[...]


def _mod_call(c_all, w_mod, b_mod):
    rows, d = c_all.shape
    n = w_mod.shape[1]
    tn = 1536
    return pl.pallas_call(
        _mod_kernel,
        grid=(n // tn,),
        in_specs=[_full((rows, d)),
                  pl.BlockSpec((d, tn), lambda j: (0, j)),
                  pl.BlockSpec((1, tn), lambda j: (0, j))],
        out_specs=pl.BlockSpec((rows, tn), lambda j: (0, j)),
        out_shape=jax.ShapeDtypeStruct((rows, n), F32),
        compiler_params=_params(("parallel",)),
        name="mod",
    )(c_all, w_mod, b_mod.reshape(1, n))


def _inproj_kernel(x_ref, sh_ref, sc_ref, w_ref, wdt_ref, v_ref, z_ref, xbc_ref, dt_ref, *,
                   conv_dim, d_inner, d_xbc):
    hb = (x_ref[0] * (1.0 + sc_ref[0]) + sh_ref[0]).astype(BF16)
    a = _dot(hb, w_ref[:, 0:conv_dim])
    g = _dot(hb, w_ref[:, conv_dim:2 * conv_dim])
    v_ref[0] = a * _sigmoid(g)
    off = 2 * conv_dim
    for j in range(d_inner // N_TILE):
        z_ref[0, :, j * N_TILE:(j + 1) * N_TILE] = _dot(hb, w_ref[:, off + j * N_TILE:off + (j + 1) * N_TILE])
    off += d_inner
    for j in range(d_xbc // N_TILE):
        xbc_ref[0, :, j * N_TILE:(j + 1) * N_TILE] = _dot(hb, w_ref[:, off + j * N_TILE:off + (j + 1) * N_TILE])
    dt_ref[0] = _dot(hb, wdt_ref[...])


def _inproj_call(x, sh, sc, w_main, w_dt, conv_dim, d_inner, d_xbc):
    b, t, d = x.shape
    tm = _token_tile(t)
    tok = lambda width: pl.BlockSpec((1, tm, width), lambda i, j: (i, j, 0))
    vec = pl.BlockSpec((1, 1, d), lambda i, j: (i, 0, 0))
    kern = functools.partial(_inproj_kernel, conv_dim=conv_dim, d_inner=d_inner, d_xbc=d_xbc)
    return pl.pallas_call(
        kern,
        grid=(b, t // tm),
        in_specs=[tok(d), vec, vec, _full(w_main.shape), _full(w_dt.shape)],
        out_specs=[tok(conv_dim), tok(d_inner), tok(d_xbc), tok(LANES)],
        out_shape=[jax.ShapeDtypeStruct((b, t, conv_dim), F32),
                   jax.ShapeDtypeStruct((b, t, d_inner), F32),
                   jax.ShapeDtypeStruct((b, t, d_xbc), F32),
                   jax.ShapeDtypeStruct((b, t, LANES), F32)],
        compiler_params=_params(("parallel", "parallel")),
        name="inproj",
    )(x, sh, sc, w_main, w_dt)


def _cconv_kernel(v_ref, w_ref, b_ref, g_ref, beta_ref, o_ref, pad_ref, acc_ref, *, seg, nseg):
    ch = v_ref.shape[-1]
    halo = 2 * SUBLANES
    for cb in range(ch // LANES):
        pad_ref[cb, 0:halo, :] = jnp.zeros((halo, LANES), F32)
        pad_ref[cb, halo + seg:2 * halo + seg, :] = jnp.zeros((halo, LANES), F32)

    def body(r, carry):
        start = pl.multiple_of(r * seg, seg)
        for cb in range(ch // LANES):
            lanes = slice(cb * LANES, (cb + 1) * LANES)
            pad_ref[cb, halo:halo + seg, :] = v_ref[0, pl.ds(start, seg), lanes]
            for rb in range(seg // CONV_ROWS):
                acc = jnp.zeros((CONV_ROWS, LANES), F32) + b_ref[:, lanes]
                for k in range(CONV_K):
                    r0 = rb * CONV_ROWS + halo - CONV_PAD + k
                    acc = acc + w_ref[k:k + 1, lanes] * pad_ref[cb, r0:r0 + CONV_ROWS, :]
                acc_ref[rb * CONV_ROWS:(rb + 1) * CONV_ROWS, lanes] = acc
        y = _layer_norm(acc_ref[...], g_ref[...], beta_ref[...])
        o_ref[0, pl.ds(start, seg), :] = (y * _sigmoid(y)).astype(BF16)
        return carry

    lax.fori_loop(0, nseg, body, 0)


def _cconv_call(v, conv_w, conv_b, ln_g, ln_b, seg):
    b, t, ch = v.shape
    tm = _token_tile(t)
    nseg = tm // seg
    tok = pl.BlockSpec((1, tm, ch), lambda i, j: (i, j, 0))
    kern = functools.partial(_cconv_kernel, seg=seg, nseg=nseg)
    return pl.pallas_call(
        kern,
        grid=(b, t // tm),
        in_specs=[tok, _full((CONV_K, ch)), _full((1, ch)), _full((1, ch)), _full((1, ch))],
        out_specs=tok,
        out_shape=jax.ShapeDtypeStruct((b, t, ch), BF16),
        scratch_shapes=[pltpu.VMEM((ch // LANES, seg + 4 * SUBLANES, LANES), F32), pltpu.VMEM((seg, ch), F32)],
        compiler_params=_params(("parallel", "parallel")),
        name="cconv",
    )(v, conv_w, conv_b.reshape(1, ch), ln_g.reshape(1, ch), ln_b.reshape(1, ch))


def _cumsum_rows(x, reverse):
    n = x.shape[0]
    row = lax.broadcasted_iota(jnp.int32, x.shape, 0)
    k = 1
    while k < n:
        if reverse:
            x = x + jnp.where(row < n - k, pltpu.roll(x, n - k, 0), 0.0)
        else:
            x = x + jnp.where(row >= k, pltpu.roll(x, k, 0), 0.0)
        k *= 2
    return x


def _expand_heads(w, e, eexp_ref):
    lane = lax.broadcasted_iota(jnp.int32, w.shape, 1)
    x = jnp.where(lane < 2 * N_HEADS, w, e)
    x_hi = x.astype(BF16).astype(F32)
    x_lo = x - x_hi
    use_hi = ((lane // N_HEADS) % 2) == 0
    lhs = jnp.where(lane < 4 * N_HEADS, jnp.where(use_hi, x_hi, x_lo), 0.0).astype(BF16)
    return _dot(lhs, eexp_ref[...])


def _expand_matrix():
    d_inner = N_HEADS * HEAD_DIM
    m = np.zeros((LANES, 2 * d_inner), np.float32)
    for j in range(4 * N_HEADS):
        h = j % N_HEADS
        base = (j // (2 * N_HEADS)) * d_inner
        m[j, base + h * HEAD_DIM:base + (h + 1) * HEAD_DIM] = 1.0
    return jnp.asarray(m, BF16)


def _state_step(h_ref, c_g, bt_g, xs_g, exp_w, exp_e, decay_row, gs):
    h_g = h_ref[:, gs]
    y_off = _dot(c_g, h_g.astype(BF16)) * exp_e[:, gs]
    xd = (xs_g * exp_w[:, gs]).astype(BF16)
    h_ref[:, gs] = h_g * decay_row[:, gs] + _dot(bt_g, xd)
    return y_off


def _ssd_a_kernel(x_ref, xp_ref, xn_ref, dt_ref, cw_ref, cb_ref, bias_f_ref, bias_b_ref,
                  alog_f_ref, alog_b_ref, dsk_f_ref, dsk_b_ref, eexp_ref, h0_ref,
                  y_ref, xact_ref, hfin_ref, pad_ref, act_ref, h_ref):
    c = pl.program_id(1)
    nc = pl.num_programs(1)
    d_inner = N_HEADS * HEAD_DIM
    gw = HEADS_PER_GROUP * HEAD_DIM

    @pl.when(c == 0)
    def _():
        h_ref[...] = h0_ref[0]

    for cbk in range(x_ref.shape[-1] // LANES):
        lanes = slice(cbk * LANES, (cbk + 1) * LANES)
        pad_ref[cbk, 0:SUBLANES, :] = jnp.where(c > 0, xp_ref[0, :, lanes], 0.0)
        pad_ref[cbk, SUBLANES:SUBLANES + CHUNK, :] = x_ref[0, :, lanes]
        pad_ref[cbk, SUBLANES + CHUNK:2 * SUBLANES + CHUNK, :] = jnp.where(c < nc - 1, xn_ref[0, :, lanes], 0.0)
        acc = jnp.zeros((CHUNK, LANES), F32) + cb_ref[:, lanes]
        for k in range(SSM_CONV_K):
            r0 = SUBLANES - SSM_PAD + k
            acc = acc + cw_ref[k:k + 1, lanes] * pad_ref[cbk, r0:r0 + CHUNK, :]
        act = acc * _sigmoid(acc)
        act_ref[:, lanes] = act
        xact_ref[0, :, lanes] = act.astype(BF16)

    raw = dt_ref[0]
    dt_f = _softplus(raw + bias_f_ref[...])
    dt_b = _softplus(raw + bias_b_ref[...])
    cs_f = _cumsum_rows(dt_f * (-jnp.exp(alog_f_ref[...])), reverse=False)
    cs_b = _cumsum_rows(dt_b * (-jnp.exp(alog_b_ref[...])), reverse=True)
    lane = lax.broadcasted_iota(jnp.int32, raw.shape, 1)
    blk = lane // N_HEADS
    stats = jnp.where(blk == 0, cs_f,
                      jnp.where(blk == 1, cs_b,
                                jnp.where(blk == 2, cs_f - jnp.log(dt_f),
                                          jnp.where(blk == 3, cs_b - jnp.log(dt_b),
                                                    jnp.log(dt_f + dt_b)))))
    stats_t = stats.T
    tot_f = cs_f[CHUNK - 1:CHUNK, :]
    exp_all = _expand_heads(dt_f * jnp.exp(tot_f - cs_f), jnp.exp(cs_f), eexp_ref)
    exp_w = exp_all[:, :d_inner]
    exp_e = exp_all[:, d_inner:]
    decay_row = exp_e[CHUNK - 1:CHUNK, :]

    row_i = lax.broadcasted_iota(jnp.int32, (CHUNK, CHUNK), 0)
    col_i = lax.broadcasted_iota(jnp.int32, (CHUNK, CHUNK), 1)
    src_before = col_i < row_i
    src_after = col_i > row_i
    left_half = col_i < HEAD_DIM
    dsk = dsk_f_ref[...] + dsk_b_ref[...]

    for g in range(SSM_GROUPS):
        gs = slice(g * gw, (g + 1) * gw)
        b_g = act_ref[:, d_inner + g * D_STATE:d_inner + (g + 1) * D_STATE]
        c_g = act_ref[:, d_inner + (SSM_GROUPS + g) * D_STATE:d_inner + (SSM_GROUPS + g + 1) * D_STATE]
        b_gb = b_g.astype(BF16)
        c_gb = c_g.astype(BF16)
        cbm = lax.dot_general(c_gb, b_gb, (((1,), (1,)), ((), ())), preferred_element_type=F32)
        xs_g = act_ref[:, gs]
        y_off = _state_step(h_ref, c_gb, b_g.T.astype(BF16), xs_g, exp_w, exp_e, decay_row, gs)
        y_pairs = []
        for pr in range(HEADS_PER_GROUP // 2):
            mats = []
            for r in range(2):
                h = g * HEADS_PER_GROUP + 2 * pr + r
                arg = jnp.where(
                    src_before, stats[:, h:h + 1] - stats_t[2 * N_HEADS + h:2 * N_HEADS + h + 1, :],
                    jnp.where(src_after,
                              stats[:, N_HEADS + h:N_HEADS + h + 1]
                              - stats_t[3 * N_HEADS + h:3 * N_HEADS + h + 1, :],
                              stats_t[4 * N_HEADS + h:4 * N_HEADS + h + 1, :]))
                mats.append((cbm * jnp.exp(arg)).astype(BF16))
            x_pair = xs_g[:, pr * LANES:(pr + 1) * LANES]
            rhs = jnp.concatenate([jnp.where(left_half, x_pair, 0.0).astype(BF16),
                                   jnp.where(left_half, 0.0, x_pair).astype(BF16)], axis=0)
            y_pairs.append(_dot(jnp.concatenate(mats, axis=1), rhs))
        y_ref[0, :, gs] = jnp.concatenate(y_pairs, axis=1) + y_off + dsk[:, gs] * xs_g

    @pl.when(c == nc - 1)
    def _():
        hfin_ref[0] = h_ref[...]


def _ssd_a_call(xbc, dt8, cw, cb, bias_f, bias_b, alog_f, alog_b, dsk_f, dsk_b, eexp, h0):
    b, t, dx = xbc.shape
    nc = t // CHUNK
    d_inner = N_HEADS * HEAD_DIM
    rows8 = CHUNK // SUBLANES
    last8 = t // SUBLANES - 1
    chunk = lambda width: pl.BlockSpec((1, CHUNK, width), lambda i, c: (i, c, 0))
    state = pl.BlockSpec((1, D_STATE, d_inner), lambda i, c: (i, 0, 0))
    return pl.pallas_call(
        _ssd_a_kernel,
        grid=(b, nc),
        in_specs=[chunk(dx),
                  pl.BlockSpec((1, SUBLANES, dx), lambda i, c: (i, jnp.maximum(c * rows8 - 1, 0), 0)),
                  pl.BlockSpec((1, SUBLANES, dx), lambda i, c: (i, jnp.minimum((c + 1) * rows8, last8), 0)),
                  chunk(LANES),
                  _full(cw.shape), _full(cb.shape), _full(bias_f.shape), _full(bias_b.shape),
                  _full(alog_f.shape), _full(alog_b.shape), _full(dsk_f.shape), _full(dsk_b.shape),
                  _full(eexp.shape), state],
        out_specs=[chunk(d_inner), chunk(dx), state],
        out_shape=[jax.ShapeDtypeStruct((b, t, d_inner), F32),
                   jax.ShapeDtypeStruct((b, t, dx), BF16),
                   jax.ShapeDtypeStruct((b, D_STATE, d_inner), F32)],
        scratch_shapes=[pltpu.VMEM((dx // LANES, CHUNK + 2 * SUBLANES, LANES), F32),
                        pltpu.VMEM((CHUNK, dx), F32),
                        pltpu.VMEM((D_STATE, d_inner), F32)],
        compiler_params=_params(("arbitrary", "arbitrary")),
        name="ssd_a",
    )(xbc, xbc, xbc, dt8, cw, cb, bias_f, bias_b, alog_f, alog_b, dsk_f, dsk_b, eexp, h0)


def _ssd_b_kernel(xact_ref, dt_ref, y_ref, z_ref, bias_b_ref, alog_b_ref, eexp_ref, nw_ref, h0_ref,
                  o_ref, hfin_ref, h_ref):
    c = pl.program_id(1)
    nc = pl.num_programs(1)
    d_inner = N_HEADS * HEAD_DIM
    gw = HEADS_PER_GROUP * HEAD_DIM

    @pl.when(c == 0)
    def _():
        h_ref[...] = h0_ref[0]

    dt_b = _softplus(dt_ref[0] + bias_b_ref[...])
    cs_b = _cumsum_rows(dt_b * (-jnp.exp(alog_b_ref[...])), reverse=True)
    tot_b = cs_b[0:1, :]
    exp_all = _expand_heads(dt_b * jnp.exp(tot_b - cs_b), jnp.exp(cs_b), eexp_ref)
    exp_w = exp_all[:, :d_inner]
    exp_e = exp_all[:, d_inner:]
    decay_row = exp_e[0:1, :]

    for g in range(SSM_GROUPS):
        gs = slice(g * gw, (g + 1) * gw)
        b_g = xact_ref[0, :, d_inner + g * D_STATE:d_inner + (g + 1) * D_STATE]
        c_g = xact_ref[0, :, d_inner + (SSM_GROUPS + g) * D_STATE:d_inner + (SSM_GROUPS + g + 1) * D_STATE]
        xs_g = xact_ref[0, :, gs].astype(F32)
        bt_g = b_g.astype(F32).T.astype(BF16)
        y = y_ref[0, :, gs] + _state_step(h_ref, c_g, bt_g, xs_g, exp_w, exp_e, decay_row, gs)
        z = z_ref[0, :, gs]
        gated = y * (z * _sigmoid(z))
        ms = jnp.mean(gated * gated, axis=-1, keepdims=True)
        o_ref[0, :, gs] = (gated * lax.rsqrt(ms + RMS_EPS) * nw_ref[:, gs]).astype(BF16)

    @pl.when(c == nc - 1)
    def _():
        hfin_ref[0] = h_ref[...]


def _ssd_b_call(xact, dt8, ypart, z, bias_b, alog_b, eexp, norm_w, h0):
    b, t, dx = xact.shape
    nc = t // CHUNK
    d_inner = N_HEADS * HEAD_DIM
    chunk = lambda width: pl.BlockSpec((1, CHUNK, width), lambda i, c: (i, nc - 1 - c, 0))
    state = pl.BlockSpec((1, D_STATE, d_inner), lambda i, c: (i, 0, 0))
    return pl.pallas_call(
        _ssd_b_kernel,
        grid=(b, nc),
        in_specs=[chunk(dx), chunk(LANES), chunk(d_inner), chunk(d_inner),
                  _full(bias_b.shape), _full(alog_b.shape), _full(eexp.shape), _full(norm_w.shape), state],
        out_specs=[chunk(d_inner), state],
        out_shape=[jax.ShapeDtypeStruct((b, t, d_inner), BF16),
                   jax.ShapeDtypeStruct((b, D_STATE, d_inner), F32)],
        scratch_shapes=[pltpu.VMEM((D_STATE, d_inner), F32)],
        compiler_params=_params(("arbitrary", "arbitrary")),
        name="ssd_b",
    )(xact, dt8, ypart, z, bias_b, alog_b, eexp, norm_w, h0)


def _outproj_kernel(cv_ref, g_ref, x_ref, gate_ref, wc_ref, ws_ref, lg_ref, lb_ref, o_ref, *, alpha):
    mix = _dot(cv_ref[0], wc_ref[...]) + _dot(g_ref[0], ws_ref[...])
    u = alpha * x_ref[0] + gate_ref[0] * mix
    o_ref[0] = _layer_norm(u, lg_ref[...], lb_ref[...])


def _outproj_call(cv, gs, x, gate, w_conv, w_ssd, ln_g, ln_b, alpha):
    b, t, d = x.shape
    tm = _token_tile(t)
    tok = lambda width: pl.BlockSpec((1, tm, width), lambda i, j: (i, j, 0))
    vec = pl.BlockSpec((1, 1, d), lambda i, j: (i, 0, 0))
    return pl.pallas_call(
        functools.partial(_outproj_kernel, alpha=alpha),
        grid=(b, t // tm),
        in_specs=[tok(cv.shape[-1]), tok(gs.shape[-1]), tok(d), vec,
                  _full(w_conv.shape), _full(w_ssd.shape), _full((1, d)), _full((1, d))],
        out_specs=tok(d),
        out_shape=jax.ShapeDtypeStruct((b, t, d), F32),
        compiler_params=_params(("parallel", "parallel")),
        name="outproj",
    )(cv, gs, x, gate, w_conv, w_ssd, ln_g.reshape(1, d), ln_b.reshape(1, d))


def _mlp_kernel(x_ref, sh_ref, sc_ref, gate_ref, w1_ref, w2_ref, lg_ref, lb_ref, o_ref, *, alpha):
    x = x_ref[0]
    hb = (x * (1.0 + sc_ref[0]) + sh_ref[0]).astype(BF16)
    acc = jnp.zeros(x.shape, F32)
    for j in range(w1_ref.shape[1] // N_TILE):
        cols = slice(j * N_TILE, (j + 1) * N_TILE)
        t = jnp.maximum(_dot(hb, w1_ref[:, cols]), 0.0)
        acc = acc + _dot((t * t).astype(BF16), w2_ref[cols, :])
    u = alpha * x + gate_ref[0] * acc
    o_ref[0] = _layer_norm(u, lg_ref[...], lb_ref[...])


def _mlp_call(x, sh, sc, gate, w1, w2, ln_g, ln_b, alpha):
    b, t, d = x.shape
    tm = _token_tile(t)
    tok = pl.BlockSpec((1, tm, d), lambda i, j: (i, j, 0))
    vec = pl.BlockSpec((1, 1, d), lambda i, j: (i, 0, 0))
    return pl.pallas_call(
        functools.partial(_mlp_kernel, alpha=alpha),
        grid=(b, t // tm),
        in_specs=[tok, vec, vec, vec, _full(w1.shape), _full(w2.shape), _full((1, d)), _full((1, d))],
        out_specs=tok,
        out_shape=jax.ShapeDtypeStruct((b, t, d), F32),
        compiler_params=_params(("parallel", "parallel")),
        name="mlp",
    )(x, sh, sc, gate, w1, w2, ln_g.reshape(1, d), ln_b.reshape(1, d))


def _mixer(x, mods, lw, h0_f, h0_b):
    sh1, sc1 = mods[0], mods[1]
    v, z, xbc, dt8 = _inproj_call(x, sh1, sc1, lw["w_main"], lw["w_dt"],
                                  lw["conv_dim"], lw["d_inner"], lw["d_xbc"])
    ypart, xact, hfin_f = _ssd_a_call(xbc, dt8, lw["ssm_cw"], lw["ssm_cb"], lw["bias_f"], lw["bias_b"],
                                      lw["alog_f"], lw["alog_b"], lw["dsk_f"], lw["dsk_b"], lw["eexp"], h0_f)
    gated, hfin_b = _ssd_b_call(xact, dt8, ypart, z, lw["bias_b"], lw["alog_b"], lw["eexp"],
                                lw["norm_w"], h0_b)
    return v, gated, hfin_f, hfin_b


def _finish_layer(x, v, gated, mods, lw, seg, alpha):
    cv = _cconv_call(v, lw["conv_w"], lw["conv_b"], lw["conv_ln_g"], lw["conv_ln_b"], seg)
    x1 = _outproj_call(cv, gated, x, mods[2], lw["w_out_conv"], lw["w_out_ssd"],
                       lw["ln1_g"], lw["ln1_b"], alpha)
    return _mlp_call(x1, mods[3], mods[4], mods[5], lw["w1"], lw["w2"], lw["ln2_g"], lw["ln2_b"], alpha)


def kernel(x, c, ctx, c_ctx, w_mod, b_mod, w_in, conv_w, conv_b, conv_ln_g, conv_ln_b, ssm_conv_w,
           ssm_conv_b, dt_bias, a_log, d_skip, ssm_norm_w, w_out, ln1_g, ln1_b, w1, w2, ln2_g, ln2_b):
    depth = w_mod.shape[0]
    bsz, _, d = x.shape
    conv_dim = conv_w.shape[-1]
    d_xbc = ssm_conv_w.shape[-1]
    d_inner = ssm_norm_w.shape[-1]
    assert d_inner == N_HEADS * HEAD_DIM and d_xbc == d_inner + 2 * SSM_GROUPS * D_STATE
    assert x.shape[1] % (CHUNK * 4) == 0 and ctx.shape[1] % CHUNK == 0
    alpha = (2 * depth) ** 0.25
    n_main = 2 * conv_dim + d_inner + d_xbc

    mod_rows = 2 * SUBLANES
    c_all = jnp.zeros((mod_rows, d), F32).at[:bsz].set(c).at[bsz].set(c_ctx)
    eexp = _expand_matrix()
    tile_heads = lambda a: jnp.tile(a.astype(F32), DT_COPIES).reshape(1, LANES)
    per_lane = lambda a: jnp.repeat(a.astype(F32), HEAD_DIM).reshape(1, d_inner)

    x_l, x_c = x, ctx
    for i in range(depth):
        last = i == depth - 1
        lw = dict(
            conv_dim=conv_dim, d_inner=d_inner, d_xbc=d_xbc,
            w_main=w_in[i][:, :n_main].astype(BF16),
            w_dt=jnp.tile(w_in[i][:, n_main:], (1, DT_COPIES)).astype(BF16),
            conv_w=conv_w[i], conv_b=conv_b[i], conv_ln_g=conv_ln_g[i], conv_ln_b=conv_ln_b[i],
            ssm_cw=ssm_conv_w[i], ssm_cb=ssm_conv_b[i].reshape(1, d_xbc),
            bias_f=tile_heads(dt_bias[i, 0]), bias_b=tile_heads(dt_bias[i, 1]),
            alog_f=tile_heads(a_log[i, 0]), alog_b=tile_heads(a_log[i, 1]),
            dsk_f=per_lane(d_skip[i, 0]), dsk_b=per_lane(d_skip[i, 1]),
            eexp=eexp, norm_w=ssm_norm_w[i].reshape(1, d_inner),
            w_out_conv=w_out[i][:conv_dim].astype(BF16), w_out_ssd=w_out[i][conv_dim:].astype(BF16),
            ln1_g=ln1_g[i], ln1_b=ln1_b[i], w1=w1[i].astype(BF16), w2=w2[i].astype(BF16),
            ln2_g=ln2_g[i], ln2_b=ln2_b[i],
        )
        mod = _mod_call(c_all, w_mod[i], b_mod[i])
        mods_l = [mod[:bsz, j * d:(j + 1) * d].reshape(bsz, 1, d) for j in range(6)]
        mods_c = [jnp.broadcast_to(mod[bsz, j * d:(j + 1) * d].reshape(1, 1, d), (bsz, 1, d)) for j in range(6)]

        h_zero = jnp.zeros((bsz, D_STATE, d_inner), F32)
        v_c, gated_c, hc_f, hc_b = _mixer(x_c, mods_c, lw, h_zero, h_zero)
        v_l, gated_l, _, _ = _mixer(x_l, mods_l, lw, hc_f, hc_b)
        x_l = _finish_layer(x_l, v_l, gated_l, mods_l, lw, GRID_W, alpha)
        if not last:
            x_c = _finish_layer(x_c, v_c, gated_c, mods_c, lw, ctx.shape[1], alpha)
    return x_l
```

```python
import functools

import numpy as np
import jax
import jax.numpy as jnp
from jax import lax
from jax.experimental import pallas as pl
from jax.experimental.pallas import tpu as pltpu

F32 = jnp.float32
BF16 = jnp.bfloat16

GRID_W = 64
CONV_K = 31
CONV_PAD = CONV_K // 2
HEAD_DIM = 64
N_HEADS = 16
SSM_GROUPS = 4
HEADS_PER_GROUP = N_HEADS // SSM_GROUPS
D_STATE = 128
SSM_CONV_K = 5
SSM_PAD = SSM_CONV_K // 2
CHUNK = 128
LN_EPS = 1e-5
RMS_EPS = 1e-5

LANES = 128
SUBLANES = 8
VMEM_LIMIT = 56 * 1024 * 1024

N_TILE = 512
CONV_ROWS = 64
CONV_HALO = 2 * SUBLANES
DT_COPIES = LANES // N_HEADS


def _token_tile(t):
    return 512 if t % 512 == 0 else 256


def _params(sem):
    return pltpu.CompilerParams(dimension_semantics=sem, vmem_limit_bytes=VMEM_LIMIT)


def _dot(a, b):
    return jnp.dot(a, b, preferred_element_type=F32)


def _sigmoid(x):
    return 1.0 / (1.0 + jnp.exp(-x))


def _softplus(x):
    return jnp.maximum(x, 0.0) + jnp.log1p(jnp.exp(-jnp.abs(x)))


def _layer_norm(u, g, b):
    mu = jnp.mean(u, axis=-1, keepdims=True)
    d = u - mu
    var = jnp.mean(d * d, axis=-1, keepdims=True)
    return d * lax.rsqrt(var + LN_EPS) * g + b


def _full(shape):
    nd = len(shape)
    return pl.BlockSpec(shape, lambda *_: (0,) * nd)


def _resident(shape):
    nd = len(shape)
    return pl.BlockSpec(shape, lambda *_: (0,) * nd, pipeline_mode=pl.Buffered(1))


def _mod_kernel(c_ref, w_ref, b_ref, o_ref):
    c = c_ref[...]
    a = c * _sigmoid(c)
    a_hi = a.astype(BF16)
    a_lo = (a - a_hi.astype(F32)).astype(BF16)
    w = w_ref[...]
    w_hi = w.astype(BF16)
    w_lo = (w - w_hi.astype(F32)).astype(BF16)
    o_ref[...] = _dot(a_hi, w_hi) + _dot(a_lo, w_hi) + _dot(a_hi, w_lo) + b_ref[...]


def _mod_call(c_all, w_mod, b_mod):
    rows, d = c_all.shape
    n = w_mod.shape[1]
    tn = 1536
    return pl.pallas_call(
        _mod_kernel,
        grid=(n // tn,),
        in_specs=[_full((rows, d)),
                  pl.BlockSpec((d, tn), lambda j: (0, j)),
                  pl.BlockSpec((1, tn), lambda j: (0, j))],
        out_specs=pl.BlockSpec((rows, tn), lambda j: (0, j)),
        out_shape=jax.ShapeDtypeStruct((rows, n), F32),
        compiler_params=_params(("parallel",)),
        name="mod",
    )(c_all, w_mod, b_mod.reshape(1, n))


def _inproj_kernel(x_ref, xp_ref, xn_ref, sh_ref, sc_ref, w_ref, wdt_ref, scw_ref, scb_ref,
                   v_ref, z_ref, xs_ref, bc_ref, dt_ref, spad_ref, *, conv_dim, d_inner, d_xbc):
    t = pl.program_id(1)
    nt = pl.num_programs(1)
    tm = x_ref.shape[1]
    scale = 1.0 + sc_ref[0]
    shift = sh_ref[0]
    h = x_ref[0] * scale + shift
    hb = h.astype(BF16)
    hb_ext = jnp.concatenate([xp_ref[0] * scale + shift, h, xn_ref[0] * scale + shift], axis=0).astype(BF16)

    a = _dot(hb, w_ref[:, 0:conv_dim])
    g = _dot(hb, w_ref[:, conv_dim:2 * conv_dim])
    v_ref[0] = a * _sigmoid(g)

    off = 2 * conv_dim
    for j in range(d_inner // N_TILE):
        z_ref[0, :, j * N_TILE:(j + 1) * N_TILE] = _dot(hb, w_ref[:, off + j * N_TILE:off + (j + 1) * N_TILE])
    off += d_inner

    tiles_per_chunk = N_TILE // LANES
    for j in range(d_xbc // N_TILE):
        res = _dot(hb_ext, w_ref[:, off + j * N_TILE:off + (j + 1) * N_TILE])
        for q in range(tiles_per_chunk):
            lt = j * tiles_per_chunk + q
            cols = slice(q * LANES, (q + 1) * LANES)
            lanes = slice(lt * LANES, (lt + 1) * LANES)
            spad_ref[lt, 0:SUBLANES, :] = jnp.where(t > 0, res[0:SUBLANES, cols], 0.0)
            spad_ref[lt, SUBLANES:SUBLANES + tm, :] = res[SUBLANES:SUBLANES + tm, cols]
            spad_ref[lt, SUBLANES + tm:2 * SUBLANES + tm, :] = jnp.where(
                t < nt - 1, res[SUBLANES + tm:2 * SUBLANES + tm, cols], 0.0)
            for rb in range(tm // CHUNK):
                acc = jnp.zeros((CHUNK, LANES), F32) + scb_ref[:, lanes]
                for k in range(SSM_CONV_K):
                    r0 = rb * CHUNK + SUBLANES - SSM_PAD + k
                    acc = acc + scw_ref[k:k + 1, lanes] * spad_ref[lt, r0:r0 + CHUNK, :]
                act = acc * _sigmoid(acc)
                rows = slice(rb * CHUNK, (rb + 1) * CHUNK)
                if lt * LANES < d_inner:
                    xs_ref[0, rows, lanes] = act
                else:
                    bc_ref[0, rows, lt * LANES - d_inner:(lt + 1) * LANES - d_inner] = act.astype(BF16)
    dt_ref[0] = _dot(hb, wdt_ref[...])


def _inproj_call(x, sh, sc, lw):
    b, t, d = x.shape
    conv_dim, d_inner, d_xbc = lw["conv_dim"], lw["d_inner"], lw["d_xbc"]
    tm = _token_tile(t)
    rows8 = tm // SUBLANES
    last8 = t // SUBLANES - 1
    tok = lambda width: pl.BlockSpec((1, tm, width), lambda i, j: (i, j, 0))
    vec = pl.BlockSpec((1, 1, d), lambda i, j: (i, 0, 0))
    kern = functools.partial(_inproj_kernel, conv_dim=conv_dim, d_inner=d_inner, d_xbc=d_xbc)
    return pl.pallas_call(
        kern,
        grid=(b, t // tm),
        in_specs=[tok(d),
                  pl.BlockSpec((1, SUBLANES, d), lambda i, j: (i, jnp.maximum(j * rows8 - 1, 0), 0)),
                  pl.BlockSpec((1, SUBLANES, d), lambda i, j: (i, jnp.minimum((j + 1) * rows8, last8), 0)),
                  vec, vec, _resident(lw["w_main"].shape), _full(lw["w_dt"].shape),
                  _full(lw["ssm_cw"].shape), _full(lw["ssm_cb"].shape)],
        out_specs=[tok(conv_dim), tok(d_inner), tok(d_inner), tok(d_xbc - d_inner), tok(LANES)],
        out_shape=[jax.ShapeDtypeStruct((b, t, conv_dim), F32),
                   jax.ShapeDtypeStruct((b, t, d_inner), F32),
                   jax.ShapeDtypeStruct((b, t, d_inner), F32),
                   jax.ShapeDtypeStruct((b, t, d_xbc - d_inner), BF16),
                   jax.ShapeDtypeStruct((b, t, LANES), F32)],
        scratch_shapes=[pltpu.VMEM((d_xbc // LANES, tm + 2 * SUBLANES, LANES), F32)],
        compiler_params=_params(("parallel", "parallel")),
        name="inproj",
    )(x, x, x, sh, sc, lw["w_main"], lw["w_dt"], lw["ssm_cw"], lw["ssm_cb"])


def _cumsum_rows(x, reverse):
    n = x.shape[0]
    row = lax.broadcasted_iota(jnp.int32, x.shape, 0)
    k = 1
    while k < n:
        if reverse:
            x = x + jnp.where(row < n - k, pltpu.roll(x, n - k, 0), 0.0)
        else:
            x = x + jnp.where(row >= k, pltpu.roll(x, k, 0), 0.0)
        k *= 2
    return x


def _expand_heads(w, e, eexp_ref):
    lane = lax.broadcasted_iota(jnp.int32, w.shape, 1)
    x = jnp.where(lane < 2 * N_HEADS, w, e)
    x_hi = x.astype(BF16).astype(F32)
    x_lo = x - x_hi
    use_hi = ((lane // N_HEADS) % 2) == 0
    lhs = jnp.where(lane < 4 * N_HEADS, jnp.where(use_hi, x_hi, x_lo), 0.0).astype(BF16)
    return _dot(lhs, eexp_ref[...])


def _expand_matrix():
    d_inner = N_HEADS * HEAD_DIM
    m = np.zeros((LANES, 2 * d_inner), np.float32)
    for j in range(4 * N_HEADS):
        h = j % N_HEADS
        base = (j // (2 * N_HEADS)) * d_inner
        m[j, base + h * HEAD_DIM:base + (h + 1) * HEAD_DIM] = 1.0
    return jnp.asarray(m, BF16)


def _state_step(h_ref, c_g, bt_g, xs_g, exp_w, exp_e, decay_row, gs):
    h_g = h_ref[:, gs]
    y_off = _dot(c_g, h_g.astype(BF16)) * exp_e[:, gs]
    xd = (xs_g * exp_w[:, gs]).astype(BF16)
    h_ref[:, gs] = h_g * decay_row[:, gs] + _dot(bt_g, xd)
    return y_off


def _group_operands(xs_ref, bc_ref, rows, g):
    gw = HEADS_PER_GROUP * HEAD_DIM
    gs = slice(g * gw, (g + 1) * gw)
    b_g = bc_ref[0, rows, g * D_STATE:(g + 1) * D_STATE]
    c_g = bc_ref[0, rows, (SSM_GROUPS + g) * D_STATE:(SSM_GROUPS + g + 1) * D_STATE]
    return gs, xs_ref[0, rows, gs], b_g, c_g, b_g.astype(F32).T.astype(BF16)


def _ssd_a_kernel(xs_ref, bc_ref, dt_ref, bias_f_ref, bias_b_ref, alog_f_ref, alog_b_ref,
                  dsk_f_ref, dsk_b_ref, eexp_ref, h0_ref, y_ref, hfin_ref, h_ref):
    step = pl.program_id(1)
    d_inner = N_HEADS * HEAD_DIM

    @pl.when(step == 0)
    def _():
        h_ref[...] = h0_ref[0]

    row_i = lax.broadcasted_iota(jnp.int32, (CHUNK, CHUNK), 0)
    col_i = lax.broadcasted_iota(jnp.int32, (CHUNK, CHUNK), 1)
    src_before = col_i < row_i
    src_after = col_i > row_i
    left_half = col_i < HEAD_DIM
    lane = col_i
    blk = lane // N_HEADS
    dsk = dsk_f_ref[...] + dsk_b_ref[...]
    a_f = -jnp.exp(alog_f_ref[...])
    a_b = -jnp.exp(alog_b_ref[...])

    for j in range(xs_ref.shape[1] // CHUNK):
        rows = slice(j * CHUNK, (j + 1) * CHUNK)
        raw = dt_ref[0, rows, :]
        dt_f = _softplus(raw + bias_f_ref[...])
        dt_b = _softplus(raw + bias_b_ref[...])
        cs_f = _cumsum_rows(dt_f * a_f, reverse=False)
        cs_b = _cumsum_rows(dt_b * a_b, reverse=True)
        stats = jnp.where(blk == 0, cs_f,
                          jnp.where(blk == 1, cs_b,
                                    jnp.where(blk == 2, cs_f - jnp.log(dt_f),
                                              jnp.where(blk == 3, cs_b - jnp.log(dt_b),
                                                        jnp.log(dt_f + dt_b)))))
        stats_t = stats.T
        tot_f = cs_f[CHUNK - 1:CHUNK, :]
        exp_all = _expand_heads(dt_f * jnp.exp(tot_f - cs_f), jnp.exp(cs_f), eexp_ref)
        exp_w = exp_all[:, :d_inner]
        exp_e = exp_all[:, d_inner:]
        decay_row = exp_e[CHUNK - 1:CHUNK, :]

        for g in range(SSM_GROUPS):
            gs, xs_g, b_g, c_g, bt_g = _group_operands(xs_ref, bc_ref, rows, g)
            cbm = lax.dot_general(c_g, b_g, (((1,), (1,)), ((), ())), preferred_element_type=F32)
            y_off = _state_step(h_ref, c_g, bt_g, xs_g, exp_w, exp_e, decay_row, gs)
            y_pairs = []
            for pr in range(HEADS_PER_GROUP // 2):
                mats = []
                for r in range(2):
                    h = g * HEADS_PER_GROUP + 2 * pr + r
                    arg = jnp.where(
                        src_before, stats[:, h:h + 1] - stats_t[2 * N_HEADS + h:2 * N_HEADS + h + 1, :],
                        jnp.where(src_after,
                                  stats[:, N_HEADS + h:N_HEADS + h + 1]
                                  - stats_t[3 * N_HEADS + h:3 * N_HEADS + h + 1, :],
                                  stats_t[4 * N_HEADS + h:4 * N_HEADS + h + 1, :]))
                    mats.append((cbm * jnp.exp(arg)).astype(BF16))
                x_pair = xs_g[:, pr * LANES:(pr + 1) * LANES]
                rhs = jnp.concatenate([jnp.where(left_half, x_pair, 0.0).astype(BF16),
                                       jnp.where(left_half, 0.0, x_pair).astype(BF16)], axis=0)
                y_pairs.append(_dot(jnp.concatenate(mats, axis=1), rhs))
            y_ref[0, rows, gs] = jnp.concatenate(y_pairs, axis=1) + y_off + dsk[:, gs] * xs_g

    @pl.when(step == pl.num_programs(1) - 1)
    def _():
        hfin_ref[0] = h_ref[...]


def _ssd_a_call(xs, bc, dt8, lw, h0):
    b, t, d_inner = xs.shape
    tc = _token_tile(t)
    tok = lambda width: pl.BlockSpec((1, tc, width), lambda i, s: (i, s, 0))
    state = pl.BlockSpec((1, D_STATE, d_inner), lambda i, s: (i, 0, 0))
    small = [lw["bias_f"], lw["bias_b"], lw["alog_f"], lw["alog_b"], lw["dsk_f"], lw["dsk_b"], lw["eexp"]]
    return pl.pallas_call(
        _ssd_a_kernel,
        grid=(b, t // tc),
        in_specs=[tok(d_inner), tok(bc.shape[-1]), tok(LANES)] + [_full(a.shape) for a in small] + [state],
        out_specs=[tok(d_inner), state],
        out_shape=[jax.ShapeDtypeStruct((b, t, d_inner), F32),
                   jax.ShapeDtypeStruct((b, D_STATE, d_inner), F32)],
        scratch_shapes=[pltpu.VMEM((D_STATE, d_inner), F32)],
        compiler_params=_params(("arbitrary", "arbitrary")),
        name="ssd_a",
    )(xs, bc, dt8, *small, h0)


def _ssd_b_kernel(xs_ref, bc_ref, dt_ref, y_ref, z_ref, bias_b_ref, alog_b_ref, eexp_ref, nw_ref, h0_ref,
                  o_ref, hfin_ref, h_ref):
    step = pl.program_id(1)
    d_inner = N_HEADS * HEAD_DIM

    @pl.when(step == 0)
    def _():
        h_ref[...] = h0_ref[0]

    a_b = -jnp.exp(alog_b_ref[...])
    for j in reversed(range(xs_ref.shape[1] // CHUNK)):
        rows = slice(j * CHUNK, (j + 1) * CHUNK)
        dt_b = _softplus(dt_ref[0, rows, :] + bias_b_ref[...])
        cs_b = _cumsum_rows(dt_b * a_b, reverse=True)
        tot_b = cs_b[0:1, :]
        exp_all = _expand_heads(dt_b * jnp.exp(tot_b - cs_b), jnp.exp(cs_b), eexp_ref)
        exp_w = exp_all[:, :d_inner]
        exp_e = exp_all[:, d_inner:]
        decay_row = exp_e[0:1, :]

        for g in range(SSM_GROUPS):
            gs, xs_g, _, c_g, bt_g = _group_operands(xs_ref, bc_ref, rows, g)
            y = y_ref[0, rows, gs] + _state_step(h_ref, c_g, bt_g, xs_g, exp_w, exp_e, decay_row, gs)
            z = z_ref[0, rows, gs]
            gated = y * (z * _sigmoid(z))
            ms = jnp.mean(gated * gated, axis=-1, keepdims=True)
            o_ref[0, rows, gs] = (gated * lax.rsqrt(ms + RMS_EPS) * nw_ref[:, gs]).astype(BF16)

    @pl.when(step == pl.num_programs(1) - 1)
    def _():
        hfin_ref[0] = h_ref[...]


def _ssd_b_call(xs, bc, dt8, ypart, z, lw, h0):
    b, t, d_inner = xs.shape
    tc = _token_tile(t)
    ns = t // tc
    tok = lambda width: pl.BlockSpec((1, tc, width), lambda i, s: (i, ns - 1 - s, 0))
    state = pl.BlockSpec((1, D_STATE, d_inner), lambda i, s: (i, 0, 0))
    small = [lw["bias_b"], lw["alog_b"], lw["eexp"], lw["norm_w"]]
    return pl.pallas_call(
        _ssd_b_kernel,
        grid=(b, ns),
        in_specs=[tok(d_inner), tok(bc.shape[-1]), tok(LANES), tok(d_inner), tok(d_inner)]
                 + [_full(a.shape) for a in small] + [state],
        out_specs=[tok(d_inner), state],
        out_shape=[jax.ShapeDtypeStruct((b, t, d_inner), BF16),
                   jax.ShapeDtypeStruct((b, D_STATE, d_inner), F32)],
        scratch_shapes=[pltpu.VMEM((D_STATE, d_inner), F32)],
        compiler_params=_params(("arbitrary", "arbitrary")),
        name="ssd_b",
    )(xs, bc, dt8, ypart, z, *small, h0)


def _tail_kernel(v_ref, g_ref, x_ref, gate1_ref, sh2_ref, sc2_ref, gate2_ref,
                 ccw_ref, ccb_ref, ccg_ref, ccbeta_ref, wc_ref, ws_ref, l1g_ref, l1b_ref,
                 w1_ref, w2_ref, l2g_ref, l2b_ref, o_ref, cpad_ref, cacc_ref, *, alpha, seg):
    tm, conv_dim = v_ref.shape[1], v_ref.shape[2]
    nseg = tm // seg
    stride = seg + 2 * CONV_HALO
    for cb in range(conv_dim // LANES):
        lanes = slice(cb * LANES, (cb + 1) * LANES)
        for s in range(nseg):
            base = s * stride
            cpad_ref[cb, base:base + CONV_HALO, :] = jnp.zeros((CONV_HALO, LANES), F32)
            cpad_ref[cb, base + CONV_HALO:base + CONV_HALO + seg, :] = v_ref[0, s * seg:(s + 1) * seg, lanes]
            cpad_ref[cb, base + CONV_HALO + seg:base + stride, :] = jnp.zeros((CONV_HALO, LANES), F32)
        for s in range(nseg):
            for rb in range(seg // CONV_ROWS):
                acc = jnp.zeros((CONV_ROWS, LANES), F32) + ccb_ref[:, lanes]
                for k in range(CONV_K):
                    r0 = s * stride + rb * CONV_ROWS + CONV_HALO - CONV_PAD + k
                    acc = acc + ccw_ref[k:k + 1, lanes] * cpad_ref[cb, r0:r0 + CONV_ROWS, :]
                r1 = s * seg + rb * CONV_ROWS
                cacc_ref[r1:r1 + CONV_ROWS, lanes] = acc
    y = _layer_norm(cacc_ref[...], ccg_ref[...], ccbeta_ref[...])
    cv = (y * _sigmoid(y)).astype(BF16)

    mix = _dot(cv, wc_ref[...]) + _dot(g_ref[0], ws_ref[...])
    x1 = _layer_norm(alpha * x_ref[0] + gate1_ref[0] * mix, l1g_ref[...], l1b_ref[...])

    hb = (x1 * (1.0 + sc2_ref[0]) + sh2_ref[0]).astype(BF16)
    acc = jnp.zeros(x1.shape, F32)
    for j in range(w1_ref.shape[1] // N_TILE):
        cols = slice(j * N_TILE, (j + 1) * N_TILE)
        t = jnp.maximum(_dot(hb, w1_ref[:, cols]), 0.0)
        acc = acc + _dot((t * t).astype(BF16), w2_ref[cols, :])
    o_ref[0] = _layer_norm(alpha * x1 + gate2_ref[0] * acc, l2g_ref[...], l2b_ref[...])


def _tail_call(v, gated, x, mods, lw, alpha, seg):
    b, t, d = x.shape
    conv_dim = v.shape[-1]
    tm = _token_tile(t)
    assert tm % seg == 0 and seg % CONV_ROWS == 0
    tok = lambda width: pl.BlockSpec((1, tm, width), lambda i, j: (i, j, 0))
    vec = pl.BlockSpec((1, 1, d), lambda i, j: (i, 0, 0))
    small = [lw["conv_w"], lw["conv_b"], lw["conv_ln_g"], lw["conv_ln_b"]]
    return pl.pallas_call(
        functools.partial(_tail_kernel, alpha=alpha, seg=seg),
        grid=(b, t // tm),
        in_specs=[tok(conv_dim), tok(gated.shape[-1]), tok(d), vec, vec, vec, vec]
                 + [_full(a.shape) for a in small]
                 + [_resident(lw["w_out_conv"].shape), _resident(lw["w_out_ssd"].shape), _full((1, d)), _full((1, d)),
                    _resident(lw["w1"].shape), _resident(lw["w2"].shape), _full((1, d)), _full((1, d))],
        out_specs=tok(d),
        out_shape=jax.ShapeDtypeStruct((b, t, d), F32),
        scratch_shapes=[pltpu.VMEM((conv_dim // LANES, (tm // seg) * (seg + 2 * CONV_HALO), LANES), F32),
                        pltpu.VMEM((tm, conv_dim), F32)],
        compiler_params=_params(("parallel", "parallel")),
        name="tail",
    )(v, gated, x, mods[2], mods[3], mods[4], mods[5], *small,
      lw["w_out_conv"], lw["w_out_ssd"], lw["ln1_g"], lw["ln1_b"], lw["w1"], lw["w2"], lw["ln2_g"], lw["ln2_b"])


def _mixer(x, mods, lw, h0_f, h0_b):
    v, z, xs, bc, dt8 = _inproj_call(x, mods[0], mods[1], lw)
    ypart, hfin_f = _ssd_a_call(xs, bc, dt8, lw, h0_f)
    gated, hfin_b = _ssd_b_call(xs, bc, dt8, ypart, z, lw, h0_b)
    return v, gated, hfin_f, hfin_b


def kernel(x, c, ctx, c_ctx, w_mod, b_mod, w_in, conv_w, conv_b, conv_ln_g, conv_ln_b, ssm_conv_w,
           ssm_conv_b, dt_bias, a_log, d_skip, ssm_norm_w, w_out, ln1_g, ln1_b, w1, w2, ln2_g, ln2_b):
    depth = w_mod.shape[0]
    bsz, _, d = x.shape
    conv_dim = conv_w.shape[-1]
    d_xbc = ssm_conv_w.shape[-1]
    d_inner = ssm_norm_w.shape[-1]
    assert d_inner == N_HEADS * HEAD_DIM and d_xbc == d_inner + 2 * SSM_GROUPS * D_STATE
    assert x.shape[1] % (CHUNK * 4) == 0 and ctx.shape[1] % CHUNK == 0
    alpha = (2 * depth) ** 0.25
    n_main = 2 * conv_dim + d_inner + d_xbc

    mod_rows = 2 * SUBLANES
    c_all = jnp.zeros((mod_rows, d), F32).at[:bsz].set(c).at[bsz].set(c_ctx)
    eexp = _expand_matrix()
    tile_heads = lambda a: jnp.tile(a.astype(F32), DT_COPIES).reshape(1, LANES)
    per_lane = lambda a: jnp.repeat(a.astype(F32), HEAD_DIM).reshape(1, d_inner)

    x_l, x_c = x, ctx
    for i in range(depth):
        last = i == depth - 1
        lw = dict(
            conv_dim=conv_dim, d_inner=d_inner, d_xbc=d_xbc,
            w_main=w_in[i][:, :n_main].astype(BF16),
            w_dt=jnp.tile(w_in[i][:, n_main:], (1, DT_COPIES)).astype(BF16),
            conv_w=conv_w[i], conv_b=conv_b[i].reshape(1, conv_dim),
            conv_ln_g=conv_ln_g[i].reshape(1, conv_dim), conv_ln_b=conv_ln_b[i].reshape(1, conv_dim),
            ssm_cw=ssm_conv_w[i], ssm_cb=ssm_conv_b[i].reshape(1, d_xbc),
            bias_f=tile_heads(dt_bias[i, 0]), bias_b=tile_heads(dt_bias[i, 1]),
            alog_f=tile_heads(a_log[i, 0]), alog_b=tile_heads(a_log[i, 1]),
            dsk_f=per_lane(d_skip[i, 0]), dsk_b=per_lane(d_skip[i, 1]),
            eexp=eexp, norm_w=ssm_norm_w[i].reshape(1, d_inner),
            w_out_conv=w_out[i][:conv_dim].astype(BF16), w_out_ssd=w_out[i][conv_dim:].astype(BF16),
            ln1_g=ln1_g[i].reshape(1, d), ln1_b=ln1_b[i].reshape(1, d),
            w1=w1[i].astype(BF16), w2=w2[i].astype(BF16),
            ln2_g=ln2_g[i].reshape(1, d), ln2_b=ln2_b[i].reshape(1, d),
        )
        mod = _mod_call(c_all, w_mod[i], b_mod[i])
        mods_l = [mod[:bsz, j * d:(j + 1) * d].reshape(bsz, 1, d) for j in range(6)]
        mods_c = [jnp.broadcast_to(mod[bsz, j * d:(j + 1) * d].reshape(1, 1, d), (bsz, 1, d)) for j in range(6)]

        h_zero = jnp.zeros((bsz, D_STATE, d_inner), F32)
        v_c, gated_c, hc_f, hc_b = _mixer(x_c, mods_c, lw, h_zero, h_zero)
        v_l, gated_l, _, _ = _mixer(x_l, mods_l, lw, hc_f, hc_b)
        x_l = _tail_call(v_l, gated_l, x_l, mods_l, lw, alpha, GRID_W)
        if not last:
            x_c = _tail_call(v_c, gated_c, x_c, mods_c, lw, alpha, ctx.shape[1])
    return x_l
```

```python
import functools

import numpy as np
import jax
import jax.numpy as jnp
from jax import lax
from jax.experimental import pallas as pl
from jax.experimental.pallas import tpu as pltpu

F32 = jnp.float32
BF16 = jnp.bfloat16

GRID_W = 64
CONV_K = 31
CONV_PAD = CONV_K // 2
HEAD_DIM = 64
N_HEADS = 16
SSM_GROUPS = 4
HEADS_PER_GROUP = N_HEADS // SSM_GROUPS
D_STATE = 128
SSM_CONV_K = 5
SSM_PAD = SSM_CONV_K // 2
CHUNK = 128
LN_EPS = 1e-5
RMS_EPS = 1e-5
LOG2_E = 1.4426950408889634

LANES = 128
SUBLANES = 8
VMEM_LIMIT = 56 * 1024 * 1024

N_TILE = 512
CONV_ROWS = 64
CONV_HALO = 2 * SUBLANES
DT_COPIES = LANES // N_HEADS


def _token_tile(t):
    return 512 if t % 512 == 0 else 256


def _params(sem, flags=None):
    return pltpu.CompilerParams(dimension_semantics=sem, vmem_limit_bytes=VMEM_LIMIT, flags=flags)


def _dot(a, b):
    return jnp.dot(a, b, preferred_element_type=F32)


def _sigmoid(x):
    return 1.0 / (1.0 + jnp.exp(-x))


def _softplus(x):
    return jnp.maximum(x, 0.0) + jnp.log1p(jnp.exp(-jnp.abs(x)))


def _layer_norm(u, g, b):
    mu = jnp.mean(u, axis=-1, keepdims=True)
    d = u - mu
    var = jnp.mean(d * d, axis=-1, keepdims=True)
    return d * lax.rsqrt(var + LN_EPS) * g + b


def _full(shape):
    nd = len(shape)
    return pl.BlockSpec(shape, lambda *_: (0,) * nd)


def _resident(shape):
    nd = len(shape)
    return pl.BlockSpec(shape, lambda *_: (0,) * nd, pipeline_mode=pl.Buffered(1))


def _mod_kernel(c_ref, w_ref, b_ref, o_ref):
    c = c_ref[...]
    a = c * _sigmoid(c)
    a_hi = a.astype(BF16)
    a_lo = (a - a_hi.astype(F32)).astype(BF16)
    w = w_ref[...]
    w_hi = w.astype(BF16)
    w_lo = (w - w_hi.astype(F32)).astype(BF16)
    o_ref[...] = _dot(a_hi, w_hi) + _dot(a_lo, w_hi) + _dot(a_hi, w_lo) + b_ref[...]


def _mod_call(c_all, w_mod, b_mod):
    rows, d = c_all.shape
    n = w_mod.shape[1]
    tn = 1536
    return pl.pallas_call(
        _mod_kernel,
        grid=(n // tn,),
        in_specs=[_full((rows, d)),
                  pl.BlockSpec((d, tn), lambda j: (0, j)),
                  pl.BlockSpec((1, tn), lambda j: (0, j))],
        out_specs=pl.BlockSpec((rows, tn), lambda j: (0, j)),
        out_shape=jax.ShapeDtypeStruct((rows, n), F32),
        compiler_params=_params(("parallel",)),
        name="mod",
    )(c_all, w_mod, b_mod.reshape(1, n))


def _inproj_kernel(x_ref, xp_ref, xn_ref, sh_ref, sc_ref, w_ref, wdt_ref, scw_ref, scb_ref,
                   v_ref, z_ref, xact_ref, dt_ref, spad_ref, *, conv_dim, d_inner, d_xbc):
    t = pl.program_id(1)
    nt = pl.num_programs(1)
    tm = x_ref.shape[1]
    scale = 1.0 + sc_ref[0]
    shift = sh_ref[0]
    h = x_ref[0] * scale + shift
    hb = h.astype(BF16)
    hb_ext = jnp.concatenate([xp_ref[0] * scale + shift, h, xn_ref[0] * scale + shift], axis=0).astype(BF16)

    a = _dot(hb, w_ref[:, 0:conv_dim])
    g = _dot(hb, w_ref[:, conv_dim:2 * conv_dim])
    v_ref[0] = a * _sigmoid(g)

    off = 2 * conv_dim
    for j in range(d_inner // N_TILE):
        z_ref[0, :, j * N_TILE:(j + 1) * N_TILE] = _dot(
            hb, w_ref[:, off + j * N_TILE:off + (j + 1) * N_TILE]).astype(BF16)
    off += d_inner

    tiles_per_chunk = N_TILE // LANES
    for j in range(d_xbc // N_TILE):
        res = _dot(hb_ext, w_ref[:, off + j * N_TILE:off + (j + 1) * N_TILE])
        for q in range(tiles_per_chunk):
            lt = j * tiles_per_chunk + q
            cols = slice(q * LANES, (q + 1) * LANES)
            lanes = slice(lt * LANES, (lt + 1) * LANES)
            spad_ref[lt, 0:SUBLANES, :] = jnp.where(t > 0, res[0:SUBLANES, cols], 0.0)
            spad_ref[lt, SUBLANES:SUBLANES + tm, :] = res[SUBLANES:SUBLANES + tm, cols]
            spad_ref[lt, SUBLANES + tm:2 * SUBLANES + tm, :] = jnp.where(
                t < nt - 1, res[SUBLANES + tm:2 * SUBLANES + tm, cols], 0.0)
            for rb in range(tm // CHUNK):
                acc = jnp.zeros((CHUNK, LANES), F32) + scb_ref[:, lanes]
                for k in range(SSM_CONV_K):
                    r0 = rb * CHUNK + SUBLANES - SSM_PAD + k
                    acc = acc + scw_ref[k:k + 1, lanes] * spad_ref[lt, r0:r0 + CHUNK, :]
                xact_ref[0, rb * CHUNK:(rb + 1) * CHUNK, lanes] = (acc * _sigmoid(acc)).astype(BF16)
    dt_ref[0] = _dot(hb, wdt_ref[...])


def _inproj_call(x, sh, sc, lw):
    b, t, d = x.shape
    conv_dim, d_inner, d_xbc = lw["conv_dim"], lw["d_inner"], lw["d_xbc"]
    tm = _token_tile(t)
    rows8 = tm // SUBLANES
    last8 = t // SUBLANES - 1
    tok = lambda width: pl.BlockSpec((1, tm, width), lambda i, j: (i, j, 0))
    vec = pl.BlockSpec((1, 1, d), lambda i, j: (i, 0, 0))
    kern = functools.partial(_inproj_kernel, conv_dim=conv_dim, d_inner=d_inner, d_xbc=d_xbc)
    return pl.pallas_call(
        kern,
        grid=(b, t // tm),
        in_specs=[tok(d),
                  pl.BlockSpec((1, SUBLANES, d), lambda i, j: (i, jnp.maximum(j * rows8 - 1, 0), 0)),
                  pl.BlockSpec((1, SUBLANES, d), lambda i, j: (i, jnp.minimum((j + 1) * rows8, last8), 0)),
                  vec, vec, _resident(lw["w_main"].shape), _full(lw["w_dt"].shape),
                  _full(lw["ssm_cw"].shape), _full(lw["ssm_cb"].shape)],
        out_specs=[tok(conv_dim), tok(d_inner), tok(d_xbc), tok(LANES)],
        out_shape=[jax.ShapeDtypeStruct((b, t, conv_dim), F32),
                   jax.ShapeDtypeStruct((b, t, d_inner), BF16),
                   jax.ShapeDtypeStruct((b, t, d_xbc), BF16),
                   jax.ShapeDtypeStruct((b, t, LANES), F32)],
        scratch_shapes=[pltpu.VMEM((d_xbc // LANES, tm + 2 * SUBLANES, LANES), F32)],
        compiler_params=_params(("parallel", "parallel")),
        name="inproj",
    )(x, x, x, sh, sc, lw["w_main"], lw["w_dt"], lw["ssm_cw"], lw["ssm_cb"])


def _cumsum_rows(x, reverse):
    n = x.shape[0]
    row = lax.broadcasted_iota(jnp.int32, x.shape, 0)
    k = 1
    while k < n:
        if reverse:
            x = x + jnp.where(row < n - k, pltpu.roll(x, n - k, 0), 0.0)
        else:
            x = x + jnp.where(row >= k, pltpu.roll(x, k, 0), 0.0)
        k *= 2
    return x


def _expand_operand(w, e):
    lane = lax.broadcasted_iota(jnp.int32, w.shape, 1)
    x = jnp.where(lane < 2 * N_HEADS, w, e)
    x_hi = x.astype(BF16).astype(F32)
    x_lo = x - x_hi
    use_hi = ((lane // N_HEADS) % 2) == 0
    return jnp.where(lane < 4 * N_HEADS, jnp.where(use_hi, x_hi, x_lo), 0.0).astype(BF16)


def _expand_matrix():
    d_inner = N_HEADS * HEAD_DIM
    m = np.zeros((LANES, 2 * d_inner), np.float32)
    for j in range(4 * N_HEADS):
        h = j % N_HEADS
        base = (j // (2 * N_HEADS)) * d_inner
        m[j, base + h * HEAD_DIM:base + (h + 1) * HEAD_DIM] = 1.0
    return jnp.asarray(m, BF16)


def _state_step(h_ref, c_g, bt_g, xs_g, exp_w, exp_e, decay_row, gs):
    h_g = h_ref[:, gs]
    y_off = _dot(c_g, h_g.astype(BF16)) * exp_e[:, gs]
    xd = (xs_g.astype(F32) * exp_w[:, gs]).astype(BF16)
    h_ref[:, gs] = h_g * decay_row[:, gs] + _dot(bt_g, xd)
    return y_off


def _group_operands(xact_ref, rows, g):
    d_inner = N_HEADS * HEAD_DIM
    gw = HEADS_PER_GROUP * HEAD_DIM
    gs = slice(g * gw, (g + 1) * gw)
    b_g = xact_ref[0, rows, d_inner + g * D_STATE:d_inner + (g + 1) * D_STATE]
    c_g = xact_ref[0, rows, d_inner + (SSM_GROUPS + g) * D_STATE:d_inner + (SSM_GROUPS + g + 1) * D_STATE]
    return gs, xact_ref[0, rows, gs], b_g, c_g, b_g.astype(F32).T.astype(BF16)


def _ssd_a_kernel(xact_ref, dt_ref, bias_f_ref, bias_b_ref, alog_f_ref, alog_b_ref,
                  dsk_f_ref, dsk_b_ref, eexp_ref, h0_ref, y_ref, lhsb_ref, hfin_ref, h_ref):
    step = pl.program_id(1)
    d_inner = N_HEADS * HEAD_DIM

    @pl.when(step == 0)
    def _():
        h_ref[...] = h0_ref[0]

    row_i = lax.broadcasted_iota(jnp.int32, (CHUNK, CHUNK), 0)
    col_i = lax.broadcasted_iota(jnp.int32, (CHUNK, CHUNK), 1)
    src_before = col_i < row_i
    src_after = col_i > row_i
    keep_left = jnp.where(col_i < HEAD_DIM, 1.0, 0.0).astype(BF16)
    keep_right = jnp.where(col_i < HEAD_DIM, 0.0, 1.0).astype(BF16)
    blk = col_i // N_HEADS
    dsk = dsk_f_ref[...] + dsk_b_ref[...]
    a_f = -jnp.exp(alog_f_ref[...])
    a_b = -jnp.exp(alog_b_ref[...])

    for j in range(xact_ref.shape[1] // CHUNK):
        rows = slice(j * CHUNK, (j + 1) * CHUNK)
        raw = dt_ref[0, rows, :]
        dt_f = _softplus(raw + bias_f_ref[...])
        dt_b = _softplus(raw + bias_b_ref[...])
        cs_f = _cumsum_rows(dt_f * a_f, reverse=False)
        cs_b = _cumsum_rows(dt_b * a_b, reverse=True)
        cs2_f = cs_f * LOG2_E
        cs2_b = cs_b * LOG2_E
        stats = jnp.where(blk == 0, cs2_f,
                          jnp.where(blk == 1, cs2_b,
                                    jnp.where(blk == 2, cs2_f - jnp.log2(dt_f),
                                              jnp.where(blk == 3, cs2_b - jnp.log2(dt_b),
                                                        jnp.log2(dt_f + dt_b)))))
        stats_t = stats.T
        tot_f = cs_f[CHUNK - 1:CHUNK, :]
        exp_all = _dot(_expand_operand(dt_f * jnp.exp(tot_f - cs_f), jnp.exp(cs_f)), eexp_ref[...])
        exp_w = exp_all[:, :d_inner]
        exp_e = exp_all[:, d_inner:]
        decay_row = exp_e[CHUNK - 1:CHUNK, :]
        tot_b = cs_b[0:1, :]
        lhsb_ref[0, rows, :] = _expand_operand(dt_b * jnp.exp(tot_b - cs_b), jnp.exp(cs_b))

        for g in range(SSM_GROUPS):
            gs, xs_g, b_g, c_g, bt_g = _group_operands(xact_ref, rows, g)
            cbm = lax.dot_general(c_g, b_g, (((1,), (1,)), ((), ())), preferred_element_type=F32)
            y_off = _state_step(h_ref, c_g, bt_g, xs_g, exp_w, exp_e, decay_row, gs)
            y_pairs = []
            for pr in range(HEADS_PER_GROUP // 2):
                mats = []
                for r in range(2):
                    h = g * HEADS_PER_GROUP + 2 * pr + r
                    arg = jnp.where(
                        src_before, stats[:, h:h + 1] - stats_t[2 * N_HEADS + h:2 * N_HEADS + h + 1, :],
                        jnp.where(src_after,
                                  stats[:, N_HEADS + h:N_HEADS + h + 1]
                                  - stats_t[3 * N_HEADS + h:3 * N_HEADS + h + 1, :],
                                  stats_t[4 * N_HEADS + h:4 * N_HEADS + h + 1, :]))
                    mats.append((cbm * jnp.exp2(arg)).astype(BF16))
                x_pair = xs_g[:, pr * LANES:(pr + 1) * LANES]
                rhs = jnp.concatenate([x_pair * keep_left, x_pair * keep_right], axis=0)
                y_pairs.append(_dot(jnp.concatenate(mats, axis=1), rhs))
            y = jnp.concatenate(y_pairs, axis=1) + y_off + dsk[:, gs] * xs_g.astype(F32)
            y_ref[0, rows, gs] = y.astype(BF16)

    @pl.when(step == pl.num_programs(1) - 1)
    def _():
        hfin_ref[0] = h_ref[...]


def _ssd_a_call(xact, dt8, lw, h0):
    b, t, dx = xact.shape
    d_inner = N_HEADS * HEAD_DIM
    tc = _token_tile(t)
    tok = lambda width: pl.BlockSpec((1, tc, width), lambda i, s: (i, s, 0))
    state = pl.BlockSpec((1, D_STATE, d_inner), lambda i, s: (i, 0, 0))
    small = [lw["bias_f"], lw["bias_b"], lw["alog_f"], lw["alog_b"], lw["dsk_f"], lw["dsk_b"], lw["eexp"]]
    return pl.pallas_call(
        _ssd_a_kernel,
        grid=(b, t // tc),
        in_specs=[tok(dx), tok(LANES)] + [_full(a.shape) for a in small] + [state],
        out_specs=[tok(d_inner), tok(LANES), state],
        out_shape=[jax.ShapeDtypeStruct((b, t, d_inner), BF16),
                   jax.ShapeDtypeStruct((b, t, LANES), BF16),
                   jax.ShapeDtypeStruct((b, D_STATE, d_inner), F32)],
        scratch_shapes=[pltpu.VMEM((D_STATE, d_inner), F32)],
        compiler_params=_params(("arbitrary", "arbitrary")),
        name="ssd_a",
    )(xact, dt8, *small, h0)


def _ssd_b_kernel(xact_ref, lhsb_ref, y_ref, z_ref, eexp_ref, nw_ref, h0_ref, o_ref, hfin_ref, h_ref):
    step = pl.program_id(1)
    d_inner = N_HEADS * HEAD_DIM

    @pl.when(step == 0)
    def _():
        h_ref[...] = h0_ref[0]

    for j in reversed(range(xact_ref.shape[1] // CHUNK)):
        rows = slice(j * CHUNK, (j + 1) * CHUNK)
        exp_all = _dot(lhsb_ref[0, rows, :], eexp_ref[...])
        exp_w = exp_all[:, :d_inner]
        exp_e = exp_all[:, d_inner:]
        decay_row = exp_e[0:1, :]

        for g in range(SSM_GROUPS):
            gs, xs_g, _, c_g, bt_g = _group_operands(xact_ref, rows, g)
            y = y_ref[0, rows, gs].astype(F32) + _state_step(h_ref, c_g, bt_g, xs_g, exp_w, exp_e, decay_row, gs)
            z = z_ref[0, rows, gs].astype(F32)
            gated = y * (z * _sigmoid(z))
            ms = jnp.mean(gated * gated, axis=-1, keepdims=True)
            o_ref[0, rows, gs] = (gated * lax.rsqrt(ms + RMS_EPS) * nw_ref[:, gs]).astype(BF16)

    @pl.when(step == pl.num_programs(1) - 1)
    def _():
        hfin_ref[0] = h_ref[...]


def _ssd_b_call(xact, lhsb, ypart, z, lw, h0):
    b, t, dx = xact.shape
    d_inner = N_HEADS * HEAD_DIM
    tc = _token_tile(t)
    ns = t // tc
    tok = lambda width: pl.BlockSpec((1, tc, width), lambda i, s: (i, ns - 1 - s, 0))
    state = pl.BlockSpec((1, D_STATE, d_inner), lambda i, s: (i, 0, 0))
    small = [lw["eexp"], lw["norm_w"]]
    return pl.pallas_call(
        _ssd_b_kernel,
        grid=(b, ns),
        in_specs=[tok(dx), tok(LANES), tok(d_inner), tok(d_inner)]
                 + [_full(a.shape) for a in small] + [state],
        out_specs=[tok(d_inner), state],
        out_shape=[jax.ShapeDtypeStruct((b, t, d_inner), BF16),
                   jax.ShapeDtypeStruct((b, D_STATE, d_inner), F32)],
        scratch_shapes=[pltpu.VMEM((D_STATE, d_inner), F32)],
        compiler_params=_params(("arbitrary", "arbitrary")),
        name="ssd_b",
    )(xact, lhsb, ypart, z, *small, h0)


def _tail_kernel(vn_ref, gn_ref, xn_ref, gate1n_ref, sh2n_ref, sc2n_ref,
                 v0_ref, g0_ref, x0_ref, gate10_ref, sh20_ref, sc20_ref, gate2_ref,
                 ccw_ref, ccb_ref, ccg_ref, ccbeta_ref, wc_ref, ws_ref, l1g_ref, l1b_ref,
                 w1_ref, w2_ref, l2g_ref, l2b_ref, o_ref,
                 cpad_ref, cwin_ref, cacc_ref, macc_ref, x1a_ref, hba_ref, x1b_ref, hbb_ref, *, alpha, seg):
    tm, conv_dim = vn_ref.shape[1], vn_ref.shape[2]
    nseg = tm // seg
    stride = seg + 2 * CONV_HALO
    n_blocks = tm // CONV_ROWS
    block_stride = stride if seg == CONV_ROWS else CONV_ROWS
    window = CONV_ROWS + 2 * CONV_HALO
    n_hidden = w1_ref.shape[0]
    step = pl.program_id(0) * pl.num_programs(1) + pl.program_id(1)

    def conv_fill(v_ref):
        for cb in range(conv_dim // LANES):
            lanes = slice(cb * LANES, (cb + 1) * LANES)
            for s in range(nseg):
                base = s * stride
                cpad_ref[cb, base:base + CONV_HALO, :] = jnp.zeros((CONV_HALO, LANES), F32)
                cpad_ref[cb, base + CONV_HALO:base + CONV_HALO + seg, :] = v_ref[0, s * seg:(s + 1) * seg, lanes]
                cpad_ref[cb, base + CONV_HALO + seg:base + stride, :] = jnp.zeros((CONV_HALO, LANES), F32)

    def conv_block(blk):
        start = pl.multiple_of(blk * block_stride, SUBLANES)
        out_rows = pl.ds(pl.multiple_of(blk * CONV_ROWS, CONV_ROWS), CONV_ROWS)
        for cb in range(conv_dim // LANES):
            lanes = slice(cb * LANES, (cb + 1) * LANES)
            cwin_ref[cb] = cpad_ref[cb, pl.ds(start, window), :]
            acc = jnp.zeros((CONV_ROWS, LANES), F32) + ccb_ref[:, lanes]
            for k in range(CONV_K):
                r0 = CONV_HALO - CONV_PAD + k
                acc = acc + ccw_ref[k:k + 1, lanes] * cwin_ref[cb, r0:r0 + CONV_ROWS, :]
            cacc_ref[out_rows, lanes] = acc

    def stage1_finish(g_ref, x_ref, gate1_ref, sh2_ref, sc2_ref, x1_out, hb_out):
        y = _layer_norm(cacc_ref[...], ccg_ref[...], ccbeta_ref[...])
        cv = (y * _sigmoid(y)).astype(BF16)
        mix = _dot(cv, wc_ref[...]) + _dot(g_ref[0], ws_ref[...])
        x1 = _layer_norm(alpha * x_ref[0] + gate1_ref[0] * mix, l1g_ref[...], l1b_ref[...])
        x1_out[...] = x1
        hb_out[...] = (x1 * (1.0 + sc2_ref[0]) + sh2_ref[0]).astype(BF16)

    def steady(x1_in, hb_in, x1_out, hb_out):
        conv_fill(vn_ref)
        macc_ref[...] = jnp.zeros(macc_ref.shape, F32)

        def body(j, carry):
            t = jnp.maximum(_dot(hb_in[...], w1_ref[j]), 0.0)
            macc_ref[...] += _dot((t * t).astype(BF16), w2_ref[j])
            conv_block(j % n_blocks)
            return carry

        lax.fori_loop(0, n_hidden, body, 0)
        o_ref[0] = _layer_norm(alpha * x1_in[...] + gate2_ref[0] * macc_ref[...], l2g_ref[...], l2b_ref[...])
        stage1_finish(gn_ref, xn_ref, gate1n_ref, sh2n_ref, sc2n_ref, x1_out, hb_out)

    @pl.when(step == 0)
    def _():
        conv_fill(v0_ref)

        def body(j, carry):
            conv_block(j)
            return carry

        lax.fori_loop(0, n_blocks, body, 0)
        stage1_finish(g0_ref, x0_ref, gate10_ref, sh20_ref, sc20_ref, x1a_ref, hba_ref)

    @pl.when(step % 2 == 0)
    def _():
        steady(x1a_ref, hba_ref, x1b_ref, hbb_ref)

    @pl.when(step % 2 == 1)
    def _():
        steady(x1b_ref, hbb_ref, x1a_ref, hba_ref)


def _tail_call(v, gated, x, mods, lw, alpha, seg):
    b, t, d = x.shape
    conv_dim = v.shape[-1]
    tm = _token_tile(t)
    nt = t // tm
    assert seg == CONV_ROWS or seg == tm
    d_ff = lw["w1"].shape[1]
    w1c = lw["w1"].reshape(d, d_ff // N_TILE, N_TILE).transpose(1, 0, 2)
    w2c = lw["w2"].reshape(d_ff // N_TILE, N_TILE, d)

    def nxt(i, j):
        wrap = (j + 1) // nt
        at_end = jnp.logical_and(wrap == 1, i == b - 1)
        return jnp.minimum(i + wrap, b - 1), jnp.where(at_end, nt - 1, (j + 1) % nt)

    tok_n = lambda width: pl.BlockSpec((1, tm, width), lambda i, j: (*nxt(i, j), 0))
    vec_n = pl.BlockSpec((1, 1, d), lambda i, j: (nxt(i, j)[0], 0, 0))
    tok_0 = lambda width: pl.BlockSpec((1, tm, width), lambda i, j: (0, 0, 0), pipeline_mode=pl.Buffered(1))
    vec_0 = pl.BlockSpec((1, 1, d), lambda i, j: (0, 0, 0))
    vec = pl.BlockSpec((1, 1, d), lambda i, j: (i, 0, 0))
    small = [lw["conv_w"], lw["conv_b"], lw["conv_ln_g"], lw["conv_ln_b"]]
    d_gated = gated.shape[-1]
    return pl.pallas_call(
        functools.partial(_tail_kernel, alpha=alpha, seg=seg),
        grid=(b, nt),
        in_specs=[tok_n(conv_dim), tok_n(d_gated), tok_n(d), vec_n, vec_n, vec_n,
                  tok_0(conv_dim), tok_0(d_gated), tok_0(d), vec_0, vec_0, vec_0, vec]
                 + [_full(a.shape) for a in small]
                 + [_resident(lw["w_out_conv"].shape), _resident(lw["w_out_ssd"].shape), _full((1, d)), _full((1, d)),
                    _resident(w1c.shape), _resident(w2c.shape), _full((1, d)), _full((1, d))],
        out_specs=pl.BlockSpec((1, tm, d), lambda i, j: (i, j, 0)),
        out_shape=jax.ShapeDtypeStruct((b, t, d), F32),
        scratch_shapes=[pltpu.VMEM((conv_dim // LANES, (tm // seg) * (seg + 2 * CONV_HALO), LANES), F32),
                        pltpu.VMEM((conv_dim // LANES, CONV_ROWS + 2 * CONV_HALO, LANES), F32),
                        pltpu.VMEM((tm, conv_dim), F32), pltpu.VMEM((tm, d), F32),
                        pltpu.VMEM((tm, d), F32), pltpu.VMEM((tm, d), BF16),
                        pltpu.VMEM((tm, d), F32), pltpu.VMEM((tm, d), BF16)],
        compiler_params=_params(("arbitrary", "arbitrary")),
        name="tail",
    )(v, gated, x, mods[2], mods[3], mods[4], v, gated, x, mods[2], mods[3], mods[4], mods[5], *small,
      lw["w_out_conv"], lw["w_out_ssd"], lw["ln1_g"], lw["ln1_b"], w1c, w2c, lw["ln2_g"], lw["ln2_b"])


def _mixer(x, mods, lw, h0_f, h0_b):
    v, z, xact, dt8 = _inproj_call(x, mods[0], mods[1], lw)
    ypart, lhsb, hfin_f = _ssd_a_call(xact, dt8, lw, h0_f)
    gated, hfin_b = _ssd_b_call(xact, lhsb, ypart, z, lw, h0_b)
    return v, gated, hfin_f, hfin_b


def kernel(x, c, ctx, c_ctx, w_mod, b_mod, w_in, conv_w, conv_b, conv_ln_g, conv_ln_b, ssm_conv_w,
           ssm_conv_b, dt_bias, a_log, d_skip, ssm_norm_w, w_out, ln1_g, ln1_b, w1, w2, ln2_g, ln2_b):
    depth = w_mod.shape[0]
    bsz, _, d = x.shape
    conv_dim = conv_w.shape[-1]
    d_xbc = ssm_conv_w.shape[-1]
    d_inner = ssm_norm_w.shape[-1]
    assert d_inner == N_HEADS * HEAD_DIM and d_xbc == d_inner + 2 * SSM_GROUPS * D_STATE
    assert x.shape[1] % (CHUNK * 4) == 0 and ctx.shape[1] % CHUNK == 0
    alpha = (2 * depth) ** 0.25
    n_main = 2 * conv_dim + d_inner + d_xbc

    mod_rows = 2 * SUBLANES
    c_all = jnp.zeros((mod_rows, d), F32).at[:bsz].set(c).at[bsz].set(c_ctx)
    eexp = _expand_matrix()
    tile_heads = lambda a: jnp.tile(a.astype(F32), DT_COPIES).reshape(1, LANES)
    per_lane = lambda a: jnp.repeat(a.astype(F32), HEAD_DIM).reshape(1, d_inner)

    x_l, x_c = x, ctx
    for i in range(depth):
        last = i == depth - 1
        lw = dict(
            conv_dim=conv_dim, d_inner=d_inner, d_xbc=d_xbc,
            w_main=w_in[i][:, :n_main].astype(BF16),
            w_dt=jnp.tile(w_in[i][:, n_main:], (1, DT_COPIES)).astype(BF16),
            conv_w=conv_w[i], conv_b=conv_b[i].reshape(1, conv_dim),
            conv_ln_g=conv_ln_g[i].reshape(1, conv_dim), conv_ln_b=conv_ln_b[i].reshape(1, conv_dim),
            ssm_cw=ssm_conv_w[i], ssm_cb=ssm_conv_b[i].reshape(1, d_xbc),
            bias_f=tile_heads(dt_bias[i, 0]), bias_b=tile_heads(dt_bias[i, 1]),
            alog_f=tile_heads(a_log[i, 0]), alog_b=tile_heads(a_log[i, 1]),
            dsk_f=per_lane(d_skip[i, 0]), dsk_b=per_lane(d_skip[i, 1]),
            eexp=eexp, norm_w=ssm_norm_w[i].reshape(1, d_inner),
            w_out_conv=w_out[i][:conv_dim].astype(BF16), w_out_ssd=w_out[i][conv_dim:].astype(BF16),
            ln1_g=ln1_g[i].reshape(1, d), ln1_b=ln1_b[i].reshape(1, d),
            w1=w1[i].astype(BF16), w2=w2[i].astype(BF16),
            ln2_g=ln2_g[i].reshape(1, d), ln2_b=ln2_b[i].reshape(1, d),
        )
        mod = _mod_call(c_all, w_mod[i], b_mod[i])
        mods_l = [mod[:bsz, j * d:(j + 1) * d].reshape(bsz, 1, d) for j in range(6)]
        mods_c = [jnp.broadcast_to(mod[bsz, j * d:(j + 1) * d].reshape(1, 1, d), (bsz, 1, d)) for j in range(6)]

        h_zero = jnp.zeros((bsz, D_STATE, d_inner), F32)
        v_c, gated_c, hc_f, hc_b = _mixer(x_c, mods_c, lw, h_zero, h_zero)
        v_l, gated_l, _, _ = _mixer(x_l, mods_l, lw, hc_f, hc_b)
        x_l = _tail_call(v_l, gated_l, x_l, mods_l, lw, alpha, GRID_W)
        if not last:
            x_c = _tail_call(v_c, gated_c, x_c, mods_c, lw, alpha, ctx.shape[1])
    return x_l
```

```python
import functools

import numpy as np
import jax
import jax.numpy as jnp
from jax import lax
from jax.experimental import pallas as pl
from jax.experimental.pallas import tpu as pltpu

F32 = jnp.float32
BF16 = jnp.bfloat16

GRID_W = 64
CONV_K = 31
CONV_PAD = CONV_K // 2
HEAD_DIM = 64
N_HEADS = 16
SSM_GROUPS = 4
HEADS_PER_GROUP = N_HEADS // SSM_GROUPS
D_STATE = 128
SSM_CONV_K = 5
SSM_PAD = SSM_CONV_K // 2
CHUNK = 128
LN_EPS = 1e-5
RMS_EPS = 1e-5
LOG2_E = 1.4426950408889634

LANES = 128
SUBLANES = 8
VMEM_LIMIT = 56 * 1024 * 1024

N_TILE = 512
CONV_ROWS = 64
CONV_HALO = 2 * SUBLANES
DT_COPIES = LANES // N_HEADS


def _token_tile(t):
    return 512 if t % 512 == 0 else 256


def _params(sem):
    return pltpu.CompilerParams(dimension_semantics=sem, vmem_limit_bytes=VMEM_LIMIT)


def _dot(a, b):
    return jnp.dot(a, b, preferred_element_type=F32)


def _sigmoid(x):
    return 1.0 / (1.0 + jnp.exp2(x * (-LOG2_E)))


def _softplus(x):
    return jnp.maximum(x, 0.0) + jnp.log1p(jnp.exp(-jnp.abs(x)))


def _layer_norm(u, g, b):
    mu = jnp.mean(u, axis=-1, keepdims=True)
    d = u - mu
    var = jnp.mean(d * d, axis=-1, keepdims=True)
    return d * lax.rsqrt(var + LN_EPS) * g + b


def _full(shape):
    nd = len(shape)
    return pl.BlockSpec(shape, lambda *_: (0,) * nd)


def _resident(shape):
    nd = len(shape)
    return pl.BlockSpec(shape, lambda *_: (0,) * nd, pipeline_mode=pl.Buffered(1))


def _mod_kernel(c_ref, w_ref, b_ref, o_ref):
    c = c_ref[...]
    a = c * _sigmoid(c)
    a_hi = a.astype(BF16)
    a_lo = (a - a_hi.astype(F32)).astype(BF16)
    w = w_ref[...]
    w_hi = w.astype(BF16)
    w_lo = (w - w_hi.astype(F32)).astype(BF16)
    o_ref[...] = _dot(a_hi, w_hi) + _dot(a_lo, w_hi) + _dot(a_hi, w_lo) + b_ref[...]


def _mod_call(c_all, w_mod, b_mod):
    rows, d = c_all.shape
    n = w_mod.shape[1]
    tn = 1536
    return pl.pallas_call(
        _mod_kernel,
        grid=(n // tn,),
        in_specs=[_full((rows, d)),
                  pl.BlockSpec((d, tn), lambda j: (0, j)),
                  pl.BlockSpec((1, tn), lambda j: (0, j))],
        out_specs=pl.BlockSpec((rows, tn), lambda j: (0, j)),
        out_shape=jax.ShapeDtypeStruct((rows, n), F32),
        compiler_params=_params(("parallel",)),
        name="mod",
    )(c_all, w_mod, b_mod.reshape(1, n))


def _inproj_kernel(x_ref, xp_ref, xn_ref, sh_ref, sc_ref, w_ref, wdt_ref, scw_ref, scb_ref,
                   v_ref, z_ref, xact_ref, dt_ref, spad_ref, *, conv_dim, d_inner, d_xbc):
    t = pl.program_id(1)
    nt = pl.num_programs(1)
    tm = x_ref.shape[1]
    scale = 1.0 + sc_ref[0]
    shift = sh_ref[0]
    h = x_ref[0] * scale + shift
    hb = h.astype(BF16)
    hb_ext = jnp.concatenate([xp_ref[0] * scale + shift, h, xn_ref[0] * scale + shift], axis=0).astype(BF16)

    a = _dot(hb, w_ref[:, 0:conv_dim])
    g = _dot(hb, w_ref[:, conv_dim:2 * conv_dim])
    v_ref[0] = a * _sigmoid(g)

    off = 2 * conv_dim
    for j in range(d_inner // N_TILE):
        z_ref[0, :, j * N_TILE:(j + 1) * N_TILE] = _dot(
            hb, w_ref[:, off + j * N_TILE:off + (j + 1) * N_TILE]).astype(BF16)
    off += d_inner

    tiles_per_chunk = N_TILE // LANES
    for j in range(d_xbc // N_TILE):
        res = _dot(hb_ext, w_ref[:, off + j * N_TILE:off + (j + 1) * N_TILE])
        for q in range(tiles_per_chunk):
            lt = j * tiles_per_chunk + q
            cols = slice(q * LANES, (q + 1) * LANES)
            lanes = slice(lt * LANES, (lt + 1) * LANES)
            spad_ref[lt, 0:SUBLANES, :] = jnp.where(t > 0, res[0:SUBLANES, cols], 0.0)
            spad_ref[lt, SUBLANES:SUBLANES + tm, :] = res[SUBLANES:SUBLANES + tm, cols]
            spad_ref[lt, SUBLANES + tm:2 * SUBLANES + tm, :] = jnp.where(
                t < nt - 1, res[SUBLANES + tm:2 * SUBLANES + tm, cols], 0.0)
            for rb in range(tm // CHUNK):
                acc = jnp.zeros((CHUNK, LANES), F32) + scb_ref[:, lanes]
                for k in range(SSM_CONV_K):
                    r0 = rb * CHUNK + SUBLANES - SSM_PAD + k
                    acc = acc + scw_ref[k:k + 1, lanes] * spad_ref[lt, r0:r0 + CHUNK, :]
                xact_ref[0, rb * CHUNK:(rb + 1) * CHUNK, lanes] = (acc * _sigmoid(acc)).astype(BF16)
    dt_ref[0] = _dot(hb, wdt_ref[...])


def _inproj_call(x, sh, sc, lw):
    b, t, d = x.shape
    conv_dim, d_inner, d_xbc = lw["conv_dim"], lw["d_inner"], lw["d_xbc"]
    tm = _token_tile(t)
    rows8 = tm // SUBLANES
    last8 = t // SUBLANES - 1
    tok = lambda width: pl.BlockSpec((1, tm, width), lambda i, j: (i, j, 0))
    vec = pl.BlockSpec((1, 1, d), lambda i, j: (i, 0, 0))
    kern = functools.partial(_inproj_kernel, conv_dim=conv_dim, d_inner=d_inner, d_xbc=d_xbc)
    return pl.pallas_call(
        kern,
        grid=(b, t // tm),
        in_specs=[tok(d),
                  pl.BlockSpec((1, SUBLANES, d), lambda i, j: (i, jnp.maximum(j * rows8 - 1, 0), 0)),
                  pl.BlockSpec((1, SUBLANES, d), lambda i, j: (i, jnp.minimum((j + 1) * rows8, last8), 0)),
                  vec, vec, _resident(lw["w_main"].shape), _full(lw["w_dt"].shape),
                  _full(lw["ssm_cw"].shape), _full(lw["ssm_cb"].shape)],
        out_specs=[tok(conv_dim), tok(d_inner), tok(d_xbc), tok(LANES)],
        out_shape=[jax.ShapeDtypeStruct((b, t, conv_dim), F32),
                   jax.ShapeDtypeStruct((b, t, d_inner), BF16),
                   jax.ShapeDtypeStruct((b, t, d_xbc), BF16),
                   jax.ShapeDtypeStruct((b, t, LANES), F32)],
        scratch_shapes=[pltpu.VMEM((d_xbc // LANES, tm + 2 * SUBLANES, LANES), F32)],
        compiler_params=_params(("parallel", "parallel")),
        name="inproj",
    )(x, x, x, sh, sc, lw["w_main"], lw["w_dt"], lw["ssm_cw"], lw["ssm_cb"])


def _cumsum_rows(x, reverse):
    n = x.shape[0]
    row = lax.broadcasted_iota(jnp.int32, x.shape, 0)
    k = 1
    while k < n:
        if reverse:
            x = x + jnp.where(row < n - k, pltpu.roll(x, n - k, 0), 0.0)
        else:
            x = x + jnp.where(row >= k, pltpu.roll(x, k, 0), 0.0)
        k *= 2
    return x


def _expand_operand(w, e):
    lane = lax.broadcasted_iota(jnp.int32, w.shape, 1)
    x = jnp.where(lane < 2 * N_HEADS, w, e)
    x_hi = x.astype(BF16).astype(F32)
    x_lo = x - x_hi
    use_hi = ((lane // N_HEADS) % 2) == 0
    return jnp.where(lane < 4 * N_HEADS, jnp.where(use_hi, x_hi, x_lo), 0.0).astype(BF16)


def _expand_matrix():
    d_inner = N_HEADS * HEAD_DIM
    m = np.zeros((LANES, 2 * d_inner), np.float32)
    for j in range(4 * N_HEADS):
        h = j % N_HEADS
        base = (j // (2 * N_HEADS)) * d_inner
        m[j, base + h * HEAD_DIM:base + (h + 1) * HEAD_DIM] = 1.0
    return jnp.asarray(m, BF16)


def _state_step(h_ref, c_g, bt_g, xs_g, exp_w, exp_e, decay_row, gs):
    h_g = h_ref[:, gs]
    y_off = _dot(c_g, h_g.astype(BF16)) * exp_e[:, gs]
    xd = (xs_g.astype(F32) * exp_w[:, gs]).astype(BF16)
    h_ref[:, gs] = h_g * decay_row[:, gs] + _dot(bt_g, xd)
    return y_off


def _group_operands(xact_ref, rows, g):
    d_inner = N_HEADS * HEAD_DIM
    gw = HEADS_PER_GROUP * HEAD_DIM
    gs = slice(g * gw, (g + 1) * gw)
    b_g = xact_ref[0, rows, d_inner + g * D_STATE:d_inner + (g + 1) * D_STATE]
    c_g = xact_ref[0, rows, d_inner + (SSM_GROUPS + g) * D_STATE:d_inner + (SSM_GROUPS + g + 1) * D_STATE]
    return gs, xact_ref[0, rows, gs], b_g, c_g, b_g.astype(F32).T.astype(BF16)


def _ssd_a_kernel(xact_ref, dt_ref, bias_f_ref, bias_b_ref, alog_f_ref, alog_b_ref,
                  dsk_f_ref, dsk_b_ref, eexp_ref, h0_ref, y_ref, lhsb_ref, hfin_ref, h_ref):
    step = pl.program_id(1)
    d_inner = N_HEADS * HEAD_DIM

    @pl.when(step == 0)
    def _():
        h_ref[...] = h0_ref[0]

    row_i = lax.broadcasted_iota(jnp.int32, (CHUNK, CHUNK), 0)
    col_i = lax.broadcasted_iota(jnp.int32, (CHUNK, CHUNK), 1)
    src_before = col_i < row_i
    src_after = col_i > row_i
    left_half = col_i < HEAD_DIM
    blk = col_i // N_HEADS
    dsk = dsk_f_ref[...] + dsk_b_ref[...]
    a_f = -jnp.exp(alog_f_ref[...])
    a_b = -jnp.exp(alog_b_ref[...])

    for j in range(xact_ref.shape[1] // CHUNK):
        rows = slice(j * CHUNK, (j + 1) * CHUNK)
        raw = dt_ref[0, rows, :]
        dt_f = _softplus(raw + bias_f_ref[...])
        dt_b = _softplus(raw + bias_b_ref[...])
        cs_f = _cumsum_rows(dt_f * a_f, reverse=False)
        cs_b = _cumsum_rows(dt_b * a_b, reverse=True)
        cs2_f = cs_f * LOG2_E
        cs2_b = cs_b * LOG2_E
        stats = jnp.where(blk == 0, cs2_f,
                          jnp.where(blk == 1, cs2_b,
                                    jnp.where(blk == 2, cs2_f - jnp.log2(dt_f),
                                              jnp.where(blk == 3, cs2_b - jnp.log2(dt_b),
                                                        jnp.log2(dt_f + dt_b)))))
        stats_t = stats.T
        tot_f = cs_f[CHUNK - 1:CHUNK, :]
        exp_all = _dot(_expand_operand(dt_f * jnp.exp(tot_f - cs_f), jnp.exp(cs_f)), eexp_ref[...])
        exp_w = exp_all[:, :d_inner]
        exp_e = exp_all[:, d_inner:]
        decay_row = exp_e[CHUNK - 1:CHUNK, :]
        tot_b = cs_b[0:1, :]
        lhsb_ref[0, rows, :] = _expand_operand(dt_b * jnp.exp(tot_b - cs_b), jnp.exp(cs_b))

        for g in range(SSM_GROUPS):
            gs, xs_g, b_g, c_g, bt_g = _group_operands(xact_ref, rows, g)
            xs_f32 = xs_g.astype(F32)
            cbm = lax.dot_general(c_g, b_g, (((1,), (1,)), ((), ())), preferred_element_type=F32)
            y_off = _state_step(h_ref, c_g, bt_g, xs_f32, exp_w, exp_e, decay_row, gs)
            y_pairs = []
            for pr in range(HEADS_PER_GROUP // 2):
                mats = []
                for r in range(2):
                    h = g * HEADS_PER_GROUP + 2 * pr + r
                    arg = jnp.where(
                        src_before, stats[:, h:h + 1] - stats_t[2 * N_HEADS + h:2 * N_HEADS + h + 1, :],
                        jnp.where(src_after,
                                  stats[:, N_HEADS + h:N_HEADS + h + 1]
                                  - stats_t[3 * N_HEADS + h:3 * N_HEADS + h + 1, :],
                                  stats_t[4 * N_HEADS + h:4 * N_HEADS + h + 1, :]))
                    mats.append((cbm * jnp.exp2(arg)).astype(BF16))
                x_pair = xs_f32[:, pr * LANES:(pr + 1) * LANES]
                rhs = jnp.concatenate([jnp.where(left_half, x_pair, 0.0).astype(BF16),
                                       jnp.where(left_half, 0.0, x_pair).astype(BF16)], axis=0)
                y_pairs.append(_dot(jnp.concatenate(mats, axis=1), rhs))
            y = jnp.concatenate(y_pairs, axis=1) + y_off + dsk[:, gs] * xs_f32
            y_ref[0, rows, gs] = y.astype(BF16)

    @pl.when(step == pl.num_programs(1) - 1)
    def _():
        hfin_ref[0] = h_ref[...]


def _ssd_a_call(xact, dt8, lw, h0):
    b, t, dx = xact.shape
    d_inner = N_HEADS * HEAD_DIM
    tc = _token_tile(t)
    tok = lambda width: pl.BlockSpec((1, tc, width), lambda i, s: (i, s, 0))
    state = pl.BlockSpec((1, D_STATE, d_inner), lambda i, s: (i, 0, 0))
    small = [lw["bias_f"], lw["bias_b"], lw["alog_f"], lw["alog_b"], lw["dsk_f"], lw["dsk_b"], lw["eexp"]]
    return pl.pallas_call(
        _ssd_a_kernel,
        grid=(b, t // tc),
        in_specs=[tok(dx), tok(LANES)] + [_full(a.shape) for a in small] + [state],
        out_specs=[tok(d_inner), tok(LANES), state],
        out_shape=[jax.ShapeDtypeStruct((b, t, d_inner), BF16),
                   jax.ShapeDtypeStruct((b, t, LANES), BF16),
                   jax.ShapeDtypeStruct((b, D_STATE, d_inner), F32)],
        scratch_shapes=[pltpu.VMEM((D_STATE, d_inner), F32)],
        compiler_params=_params(("arbitrary", "arbitrary")),
        name="ssd_a",
    )(xact, dt8, *small, h0)


def _ssd_b_kernel(xact_ref, lhsb_ref, y_ref, z_ref, eexp_ref, nw_ref, h0_ref, o_ref, hfin_ref, h_ref):
    step = pl.program_id(1)
    d_inner = N_HEADS * HEAD_DIM

    @pl.when(step == 0)
    def _():
        h_ref[...] = h0_ref[0]

    for j in reversed(range(xact_ref.shape[1] // CHUNK)):
        rows = slice(j * CHUNK, (j + 1) * CHUNK)
        exp_all = _dot(lhsb_ref[0, rows, :], eexp_ref[...])
        exp_w = exp_all[:, :d_inner]
        exp_e = exp_all[:, d_inner:]
        decay_row = exp_e[0:1, :]

        for g in range(SSM_GROUPS):
            gs, xs_g, _, c_g, bt_g = _group_operands(xact_ref, rows, g)
            y = y_ref[0, rows, gs].astype(F32) + _state_step(h_ref, c_g, bt_g, xs_g, exp_w, exp_e, decay_row, gs)
            z = z_ref[0, rows, gs].astype(F32)
            gated = y * (z * _sigmoid(z))
            ms = jnp.mean(gated * gated, axis=-1, keepdims=True)
            o_ref[0, rows, gs] = (gated * lax.rsqrt(ms + RMS_EPS) * nw_ref[:, gs]).astype(BF16)

    @pl.when(step == pl.num_programs(1) - 1)
    def _():
        hfin_ref[0] = h_ref[...]


def _ssd_b_call(xact, lhsb, ypart, z, lw, h0):
    b, t, dx = xact.shape
    d_inner = N_HEADS * HEAD_DIM
    tc = _token_tile(t)
    ns = t // tc
    tok = lambda width: pl.BlockSpec((1, tc, width), lambda i, s: (i, ns - 1 - s, 0))
    state = pl.BlockSpec((1, D_STATE, d_inner), lambda i, s: (i, 0, 0))
    small = [lw["eexp"], lw["norm_w"]]
    return pl.pallas_call(
        _ssd_b_kernel,
        grid=(b, ns),
        in_specs=[tok(dx), tok(LANES), tok(d_inner), tok(d_inner)]
                 + [_full(a.shape) for a in small] + [state],
        out_specs=[tok(d_inner), state],
        out_shape=[jax.ShapeDtypeStruct((b, t, d_inner), BF16),
                   jax.ShapeDtypeStruct((b, D_STATE, d_inner), F32)],
        scratch_shapes=[pltpu.VMEM((D_STATE, d_inner), F32)],
        compiler_params=_params(("arbitrary", "arbitrary")),
        name="ssd_b",
    )(xact, lhsb, ypart, z, *small, h0)


def _tail_kernel(v_ref, g_ref, x_ref, gate1_ref, sh2_ref, sc2_ref, gate2_ref,
                 ccw_ref, ccb_ref, ccg_ref, ccbeta_ref, wc_ref, ws_ref, l1g_ref, l1b_ref,
                 w1_ref, w2_ref, l2g_ref, l2b_ref, o_ref, cpad_ref, cacc_ref, *, alpha, seg):
    tm, conv_dim = v_ref.shape[1], v_ref.shape[2]
    nseg = tm // seg
    stride = seg + 2 * CONV_HALO
    for cb in range(conv_dim // LANES):
        lanes = slice(cb * LANES, (cb + 1) * LANES)
        for s in range(nseg):
            base = s * stride
            cpad_ref[cb, base:base + CONV_HALO, :] = jnp.zeros((CONV_HALO, LANES), F32)
            cpad_ref[cb, base + CONV_HALO:base + CONV_HALO + seg, :] = v_ref[0, s * seg:(s + 1) * seg, lanes]
            cpad_ref[cb, base + CONV_HALO + seg:base + stride, :] = jnp.zeros((CONV_HALO, LANES), F32)
        for s in range(nseg):
            for rb in range(seg // CONV_ROWS):
                acc = jnp.zeros((CONV_ROWS, LANES), F32) + ccb_ref[:, lanes]
                for k in range(CONV_K):
                    r0 = s * stride + rb * CONV_ROWS + CONV_HALO - CONV_PAD + k
                    acc = acc + ccw_ref[k:k + 1, lanes] * cpad_ref[cb, r0:r0 + CONV_ROWS, :]
                r1 = s * seg + rb * CONV_ROWS
                cacc_ref[r1:r1 + CONV_ROWS, lanes] = acc
    y = _layer_norm(cacc_ref[...], ccg_ref[...], ccbeta_ref[...])
    cv = (y * _sigmoid(y)).astype(BF16)

    mix = _dot(cv, wc_ref[...]) + _dot(g_ref[0], ws_ref[...])
    x1 = _layer_norm(alpha * x_ref[0] + gate1_ref[0] * mix, l1g_ref[...], l1b_ref[...])

    hb = (x1 * (1.0 + sc2_ref[0]) + sh2_ref[0]).astype(BF16)
    acc = jnp.zeros(x1.shape, F32)
    for j in range(w1_ref.shape[1] // N_TILE):
        cols = slice(j * N_TILE, (j + 1) * N_TILE)
        t = jnp.maximum(_dot(hb, w1_ref[:, cols]), 0.0)
        acc = acc + _dot((t * t).astype(BF16), w2_ref[cols, :])
    o_ref[0] = _layer_norm(alpha * x1 + gate2_ref[0] * acc, l2g_ref[...], l2b_ref[...])


def _tail_call(v, gated, x, mods, lw, alpha, seg):
    b, t, d = x.shape
    conv_dim = v.shape[-1]
    tm = _token_tile(t)
    assert tm % seg == 0 and seg % CONV_ROWS == 0
    tok = lambda width: pl.BlockSpec((1, tm, width), lambda i, j: (i, j, 0))
    vec = pl.BlockSpec((1, 1, d), lambda i, j: (i, 0, 0))
    small = [lw["conv_w"], lw["conv_b"], lw["conv_ln_g"], lw["conv_ln_b"]]
    return pl.pallas_call(
        functools.partial(_tail_kernel, alpha=alpha, seg=seg),
        grid=(b, t // tm),
        in_specs=[tok(conv_dim), tok(gated.shape[-1]), tok(d), vec, vec, vec, vec]
                 + [_full(a.shape) for a in small]
                 + [_resident(lw["w_out_conv"].shape), _resident(lw["w_out_ssd"].shape), _full((1, d)), _full((1, d)),
                    _resident(lw["w1"].shape), _resident(lw["w2"].shape), _full((1, d)), _full((1, d))],
        out_specs=tok(d),
        out_shape=jax.ShapeDtypeStruct((b, t, d), F32),
        scratch_shapes=[pltpu.VMEM((conv_dim // LANES, (tm // seg) * (seg + 2 * CONV_HALO), LANES), F32),
                        pltpu.VMEM((tm, conv_dim), F32)],
        compiler_params=_params(("parallel", "parallel")),
        name="tail",
    )(v, gated, x, mods[2], mods[3], mods[4], mods[5], *small,
      lw["w_out_conv"], lw["w_out_ssd"], lw["ln1_g"], lw["ln1_b"], lw["w1"], lw["w2"], lw["ln2_g"], lw["ln2_b"])


def _mixer(x, mods, lw, h0_f, h0_b):
    v, z, xact, dt8 = _inproj_call(x, mods[0], mods[1], lw)
    ypart, lhsb, hfin_f = _ssd_a_call(xact, dt8, lw, h0_f)
    gated, hfin_b = _ssd_b_call(xact, lhsb, ypart, z, lw, h0_b)
    return v, gated, hfin_f, hfin_b


def kernel(x, c, ctx, c_ctx, w_mod, b_mod, w_in, conv_w, conv_b, conv_ln_g, conv_ln_b, ssm_conv_w,
           ssm_conv_b, dt_bias, a_log, d_skip, ssm_norm_w, w_out, ln1_g, ln1_b, w1, w2, ln2_g, ln2_b):
    depth = w_mod.shape[0]
    bsz, _, d = x.shape
    conv_dim = conv_w.shape[-1]
    d_xbc = ssm_conv_w.shape[-1]
    d_inner = ssm_norm_w.shape[-1]
    assert d_inner == N_HEADS * HEAD_DIM and d_xbc == d_inner + 2 * SSM_GROUPS * D_STATE
    assert x.shape[1] % (CHUNK * 4) == 0 and ctx.shape[1] % CHUNK == 0
    alpha = (2 * depth) ** 0.25
    n_main = 2 * conv_dim + d_inner + d_xbc

    mod_rows = 2 * SUBLANES
    c_all = jnp.zeros((mod_rows, d), F32).at[:bsz].set(c).at[bsz].set(c_ctx)
    eexp = _expand_matrix()
    tile_heads = lambda a: jnp.tile(a.astype(F32), DT_COPIES).reshape(1, LANES)
    per_lane = lambda a: jnp.repeat(a.astype(F32), HEAD_DIM).reshape(1, d_inner)

    x_l, x_c = x, ctx
    for i in range(depth):
        last = i == depth - 1
        lw = dict(
            conv_dim=conv_dim, d_inner=d_inner, d_xbc=d_xbc,
            w_main=w_in[i][:, :n_main].astype(BF16),
            w_dt=jnp.tile(w_in[i][:, n_main:], (1, DT_COPIES)).astype(BF16),
            conv_w=conv_w[i], conv_b=conv_b[i].reshape(1, conv_dim),
            conv_ln_g=conv_ln_g[i].reshape(1, conv_dim), conv_ln_b=conv_ln_b[i].reshape(1, conv_dim),
            ssm_cw=ssm_conv_w[i], ssm_cb=ssm_conv_b[i].reshape(1, d_xbc),
            bias_f=tile_heads(dt_bias[i, 0]), bias_b=tile_heads(dt_bias[i, 1]),
            alog_f=tile_heads(a_log[i, 0]), alog_b=tile_heads(a_log[i, 1]),
            dsk_f=per_lane(d_skip[i, 0]), dsk_b=per_lane(d_skip[i, 1]),
            eexp=eexp, norm_w=ssm_norm_w[i].reshape(1, d_inner),
            w_out_conv=w_out[i][:conv_dim].astype(BF16), w_out_ssd=w_out[i][conv_dim:].astype(BF16),
            ln1_g=ln1_g[i].reshape(1, d), ln1_b=ln1_b[i].reshape(1, d),
            w1=w1[i].astype(BF16), w2=w2[i].astype(BF16),
            ln2_g=ln2_g[i].reshape(1, d), ln2_b=ln2_b[i].reshape(1, d),
        )
        mod = _mod_call(c_all, w_mod[i], b_mod[i])
        mods_l = [mod[:bsz, j * d:(j + 1) * d].reshape(bsz, 1, d) for j in range(6)]
        mods_c = [jnp.broadcast_to(mod[bsz, j * d:(j + 1) * d].reshape(1, 1, d), (bsz, 1, d)) for j in range(6)]

        h_zero = jnp.zeros((bsz, D_STATE, d_inner), F32)
        v_c, gated_c, hc_f, hc_b = _mixer(x_c, mods_c, lw, h_zero, h_zero)
        v_l, gated_l, _, _ = _mixer(x_l, mods_l, lw, hc_f, hc_b)
        x_l = _tail_call(v_l, gated_l, x_l, mods_l, lw, alpha, GRID_W)
        if not last:
            x_c = _tail_call(v_c, gated_c, x_c, mods_c, lw, alpha, ctx.shape[1])
    return x_l
```

```python
import functools

import numpy as np
import jax
import jax.numpy as jnp
from jax import lax
from jax.experimental import pallas as pl
from jax.experimental.pallas import tpu as pltpu

F32 = jnp.float32
BF16 = jnp.bfloat16

GRID_W = 64
CONV_K = 31
CONV_PAD = CONV_K // 2
HEAD_DIM = 64
N_HEADS = 16
SSM_GROUPS = 4
HEADS_PER_GROUP = N_HEADS // SSM_GROUPS
D_STATE = 128
SSM_CONV_K = 5
SSM_PAD = SSM_CONV_K // 2
CHUNK = 128
LN_EPS = 1e-5
RMS_EPS = 1e-5
LOG2_E = 1.4426950408889634

LANES = 128
SUBLANES = 8
VMEM_LIMIT = 56 * 1024 * 1024

N_TILE = 512
CONV_ROWS = 64
CONV_HALO = 2 * SUBLANES
DT_COPIES = LANES // N_HEADS


def _token_tile(t):
    return 512 if t % 512 == 0 else 256


def _scan_tile(t):
    return 1024 if t % 1024 == 0 else 256


def _params(sem):
    return pltpu.CompilerParams(dimension_semantics=sem, vmem_limit_bytes=VMEM_LIMIT)


def _dot(a, b):
    return jnp.dot(a, b, preferred_element_type=F32)


def _sigmoid(x):
    return 1.0 / (1.0 + jnp.exp2(x * (-LOG2_E)))


def _softplus(x):
    return jnp.maximum(x, 0.0) + jnp.log1p(jnp.exp(-jnp.abs(x)))


def _layer_norm(u, g, b):
    mu = jnp.mean(u, axis=-1, keepdims=True)
    d = u - mu
    var = jnp.mean(d * d, axis=-1, keepdims=True)
    return d * lax.rsqrt(var + LN_EPS) * g + b


def _full(shape):
    nd = len(shape)
    return pl.BlockSpec(shape, lambda *_: (0,) * nd)


def _resident(shape):
    nd = len(shape)
    return pl.BlockSpec(shape, lambda *_: (0,) * nd, pipeline_mode=pl.Buffered(1))


def _mod_kernel(c_ref, w_ref, b_ref, o_ref):
    c = c_ref[...]
    a = c * _sigmoid(c)
    a_hi = a.astype(BF16)
    a_lo = (a - a_hi.astype(F32)).astype(BF16)
    w = w_ref[...]
    w_hi = w.astype(BF16)
    w_lo = (w - w_hi.astype(F32)).astype(BF16)
    o_ref[...] = _dot(a_hi, w_hi) + _dot(a_lo, w_hi) + _dot(a_hi, w_lo) + b_ref[...]


def _mod_call(c_all, w_mod, b_mod, layer):
    rows, d = c_all.shape
    depth, _, n = w_mod.shape
    tn = 1536
    return pl.pallas_call(
        _mod_kernel,
        grid=(n // tn,),
        in_specs=[_full((rows, d)),
                  pl.BlockSpec((None, d, tn), lambda j: (layer, 0, j)),
                  pl.BlockSpec((None, 1, tn), lambda j: (layer, 0, j))],
        out_specs=pl.BlockSpec((rows, tn), lambda j: (0, j)),
        out_shape=jax.ShapeDtypeStruct((rows, n), F32),
        compiler_params=_params(("parallel",)),
        name="mod",
    )(c_all, w_mod, b_mod.reshape(depth, 1, n))


def _inproj_kernel(x_ref, xp_ref, xn_ref, sh_ref, sc_ref, w_ref, wdt_ref, scw_ref, scb_ref,
                   v_ref, z_ref, xact_ref, dt_ref, spad_ref, *, conv_dim, d_inner, d_xbc):
    t = pl.program_id(1)
    nt = pl.num_programs(1)
    tm = x_ref.shape[1]
    scale = 1.0 + sc_ref[0]
    shift = sh_ref[0]
    h = x_ref[0] * scale + shift
    hb = h.astype(BF16)
    hb_ext = jnp.concatenate([xp_ref[0] * scale + shift, h, xn_ref[0] * scale + shift], axis=0).astype(BF16)

    a = _dot(hb, w_ref[:, 0:conv_dim])
    g = _dot(hb, w_ref[:, conv_dim:2 * conv_dim])
    v_ref[0] = a * _sigmoid(g)

    off = 2 * conv_dim
    for j in range(d_inner // N_TILE):
        z_ref[0, :, j * N_TILE:(j + 1) * N_TILE] = _dot(
            hb, w_ref[:, off + j * N_TILE:off + (j + 1) * N_TILE]).astype(BF16)
    off += d_inner

    tiles_per_chunk = N_TILE // LANES
    for j in range(d_xbc // N_TILE):
        res = _dot(hb_ext, w_ref[:, off + j * N_TILE:off + (j + 1) * N_TILE])
        for q in range(tiles_per_chunk):
            lt = j * tiles_per_chunk + q
            cols = slice(q * LANES, (q + 1) * LANES)
            lanes = slice(lt * LANES, (lt + 1) * LANES)
            spad_ref[lt, 0:SUBLANES, :] = jnp.where(t > 0, res[0:SUBLANES, cols], 0.0)
            spad_ref[lt, SUBLANES:SUBLANES + tm, :] = res[SUBLANES:SUBLANES + tm, cols]
            spad_ref[lt, SUBLANES + tm:2 * SUBLANES + tm, :] = jnp.where(
                t < nt - 1, res[SUBLANES + tm:2 * SUBLANES + tm, cols], 0.0)
            for rb in range(tm // CHUNK):
                acc = jnp.zeros((CHUNK, LANES), F32) + scb_ref[:, lanes]
                for k in range(SSM_CONV_K):
                    r0 = rb * CHUNK + SUBLANES - SSM_PAD + k
                    acc = acc + scw_ref[k:k + 1, lanes] * spad_ref[lt, r0:r0 + CHUNK, :]
                xact_ref[0, rb * CHUNK:(rb + 1) * CHUNK, lanes] = (acc * _sigmoid(acc)).astype(BF16)
    dt_ref[0] = _dot(hb, wdt_ref[...])


def _inproj_call(x, sh, sc, lw):
    b, t, d = x.shape
    conv_dim, d_inner, d_xbc = lw["conv_dim"], lw["d_inner"], lw["d_xbc"]
    tm = _token_tile(t)
    rows8 = tm // SUBLANES
    last8 = t // SUBLANES - 1
    tok = lambda width: pl.BlockSpec((1, tm, width), lambda i, j: (i, j, 0))
    vec = pl.BlockSpec((1, 1, d), lambda i, j: (i, 0, 0))
    kern = functools.partial(_inproj_kernel, conv_dim=conv_dim, d_inner=d_inner, d_xbc=d_xbc)
    return pl.pallas_call(
        kern,
        grid=(b, t // tm),
        in_specs=[tok(d),
                  pl.BlockSpec((1, SUBLANES, d), lambda i, j: (i, jnp.maximum(j * rows8 - 1, 0), 0)),
                  pl.BlockSpec((1, SUBLANES, d), lambda i, j: (i, jnp.minimum((j + 1) * rows8, last8), 0)),
                  vec, vec, _resident(lw["w_main"].shape), _full(lw["w_dt"].shape),
                  _full(lw["ssm_cw"].shape), _full(lw["ssm_cb"].shape)],
        out_specs=[tok(conv_dim), tok(d_inner), tok(d_xbc), tok(LANES)],
        out_shape=[jax.ShapeDtypeStruct((b, t, conv_dim), F32),
                   jax.ShapeDtypeStruct((b, t, d_inner), BF16),
                   jax.ShapeDtypeStruct((b, t, d_xbc), BF16),
                   jax.ShapeDtypeStruct((b, t, LANES), F32)],
        scratch_shapes=[pltpu.VMEM((d_xbc // LANES, tm + 2 * SUBLANES, LANES), F32)],
        compiler_params=_params(("parallel", "parallel")),
        name="inproj",
    )(x, x, x, sh, sc, lw["w_main"], lw["w_dt"], lw["ssm_cw"], lw["ssm_cb"])


def _cumsum_rows(x, reverse):
    n = x.shape[0]
    row = lax.broadcasted_iota(jnp.int32, x.shape, 0)
    k = 1
    while k < n:
        if reverse:
            x = x + jnp.where(row < n - k, pltpu.roll(x, n - k, 0), 0.0)
        else:
            x = x + jnp.where(row >= k, pltpu.roll(x, k, 0), 0.0)
        k *= 2
    return x


def _expand_operand(w, e):
    lane = lax.broadcasted_iota(jnp.int32, w.shape, 1)
    x = jnp.where(lane < 2 * N_HEADS, w, e)
    x_hi = x.astype(BF16).astype(F32)
    x_lo = x - x_hi
    use_hi = ((lane // N_HEADS) % 2) == 0
    return jnp.where(lane < 4 * N_HEADS, jnp.where(use_hi, x_hi, x_lo), 0.0).astype(BF16)


def _expand_matrix():
    d_inner = N_HEADS * HEAD_DIM
    m = np.zeros((LANES, 2 * d_inner), np.float32)
    for j in range(4 * N_HEADS):
        h = j % N_HEADS
        base = (j // (2 * N_HEADS)) * d_inner
        m[j, base + h * HEAD_DIM:base + (h + 1) * HEAD_DIM] = 1.0
    return jnp.asarray(m, BF16)


def _state_step(h_ref, c_g, bt_g, xs_g, exp_w, exp_e, decay_row, gs):
    h_g = h_ref[:, gs]
    y_off = _dot(c_g, h_g.astype(BF16)) * exp_e[:, gs]
    xd = (xs_g.astype(F32) * exp_w[:, gs]).astype(BF16)
    h_ref[:, gs] = h_g * decay_row[:, gs] + _dot(bt_g, xd)
    return y_off


def _group_operands(xact_ref, rows, g):
    d_inner = N_HEADS * HEAD_DIM
    gw = HEADS_PER_GROUP * HEAD_DIM
    gs = slice(g * gw, (g + 1) * gw)
    b_g = xact_ref[0, rows, d_inner + g * D_STATE:d_inner + (g + 1) * D_STATE]
    c_g = xact_ref[0, rows, d_inner + (SSM_GROUPS + g) * D_STATE:d_inner + (SSM_GROUPS + g + 1) * D_STATE]
    return gs, xact_ref[0, rows, gs], b_g, c_g, b_g.astype(F32).T.astype(BF16)


def _ssd_a_kernel(xact_ref, dt_ref, bias_f_ref, bias_b_ref, alog_f_ref, alog_b_ref,
                  dsk_f_ref, dsk_b_ref, eexp_ref, h0_ref, y_ref, lhsb_ref, hfin_ref, h_ref):
    step = pl.program_id(1)
    d_inner = N_HEADS * HEAD_DIM

    @pl.when(step == 0)
    def _():
        h_ref[...] = h0_ref[0]

    row_i = lax.broadcasted_iota(jnp.int32, (CHUNK, CHUNK), 0)
    col_i = lax.broadcasted_iota(jnp.int32, (CHUNK, CHUNK), 1)
    src_before = col_i < row_i
    src_after = col_i > row_i
    left_half = col_i < HEAD_DIM
    blk = col_i // N_HEADS
    dsk = dsk_f_ref[...] + dsk_b_ref[...]
    a_f = -jnp.exp(alog_f_ref[...])
    a_b = -jnp.exp(alog_b_ref[...])

    for j in range(xact_ref.shape[1] // CHUNK):
        rows = slice(j * CHUNK, (j + 1) * CHUNK)
        raw = dt_ref[0, rows, :]
        dt_f = _softplus(raw + bias_f_ref[...])
        dt_b = _softplus(raw + bias_b_ref[...])
        cs_f = _cumsum_rows(dt_f * a_f, reverse=False)
        cs_b = _cumsum_rows(dt_b * a_b, reverse=True)
        cs2_f = cs_f * LOG2_E
        cs2_b = cs_b * LOG2_E
        stats = jnp.where(blk == 0, cs2_f,
                          jnp.where(blk == 1, cs2_b,
                                    jnp.where(blk == 2, cs2_f - jnp.log2(dt_f),
                                              jnp.where(blk == 3, cs2_b - jnp.log2(dt_b),
                                                        jnp.log2(dt_f + dt_b)))))
        stats_t = stats.T
        tot_f = cs_f[CHUNK - 1:CHUNK, :]
        exp_all = _dot(_expand_operand(dt_f * jnp.exp(tot_f - cs_f), jnp.exp(cs_f)), eexp_ref[...])
        exp_w = exp_all[:, :d_inner]
        exp_e = exp_all[:, d_inner:]
        decay_row = exp_e[CHUNK - 1:CHUNK, :]
        tot_b = cs_b[0:1, :]
        lhsb_ref[0, rows, :] = _expand_operand(dt_b * jnp.exp(tot_b - cs_b), jnp.exp(cs_b))

        for g in range(SSM_GROUPS):
            gs, xs_g, b_g, c_g, bt_g = _group_operands(xact_ref, rows, g)
            xs_f32 = xs_g.astype(F32)
            cbm = lax.dot_general(c_g, b_g, (((1,), (1,)), ((), ())), preferred_element_type=F32)
            y_off = _state_step(h_ref, c_g, bt_g, xs_f32, exp_w, exp_e, decay_row, gs)
            y_pairs = []
            for pr in range(HEADS_PER_GROUP // 2):
                mats = []
                for r in range(2):
                    h = g * HEADS_PER_GROUP + 2 * pr + r
                    arg = jnp.where(
                        src_before, stats[:, h:h + 1] - stats_t[2 * N_HEADS + h:2 * N_HEADS + h + 1, :],
                        jnp.where(src_after,
                                  stats[:, N_HEADS + h:N_HEADS + h + 1]
                                  - stats_t[3 * N_HEADS + h:3 * N_HEADS + h + 1, :],
                                  stats_t[4 * N_HEADS + h:4 * N_HEADS + h + 1, :]))
                    mats.append((cbm * jnp.exp2(arg)).astype(BF16))
                x_pair = xs_f32[:, pr * LANES:(pr + 1) * LANES]
                rhs = jnp.concatenate([jnp.where(left_half, x_pair, 0.0).astype(BF16),
                                       jnp.where(left_half, 0.0, x_pair).astype(BF16)], axis=0)
                y_pairs.append(_dot(jnp.concatenate(mats, axis=1), rhs))
            y = jnp.concatenate(y_pairs, axis=1) + y_off + dsk[:, gs] * xs_f32
            y_ref[0, rows, gs] = y.astype(BF16)

    @pl.when(step == pl.num_programs(1) - 1)
    def _():
        hfin_ref[0] = h_ref[...]


def _ssd_a_call(xact, dt8, lw, h0):
    b, t, dx = xact.shape
    d_inner = N_HEADS * HEAD_DIM
    tc = _scan_tile(t)
    tok = lambda width: pl.BlockSpec((1, tc, width), lambda i, s: (i, s, 0))
    state = pl.BlockSpec((1, D_STATE, d_inner), lambda i, s: (i, 0, 0))
    small = [lw["bias_f"], lw["bias_b"], lw["alog_f"], lw["alog_b"], lw["dsk_f"], lw["dsk_b"], lw["eexp"]]
    return pl.pallas_call(
        _ssd_a_kernel,
        grid=(b, t // tc),
        in_specs=[tok(dx), tok(LANES)] + [_full(a.shape) for a in small] + [state],
        out_specs=[tok(d_inner), tok(LANES), state],
        out_shape=[jax.ShapeDtypeStruct((b, t, d_inner), BF16),
                   jax.ShapeDtypeStruct((b, t, LANES), BF16),
                   jax.ShapeDtypeStruct((b, D_STATE, d_inner), F32)],
        scratch_shapes=[pltpu.VMEM((D_STATE, d_inner), F32)],
        compiler_params=_params(("arbitrary", "arbitrary")),
        name="ssd_a",
    )(xact, dt8, *small, h0)


def _ssd_b_kernel(xact_ref, lhsb_ref, y_ref, z_ref, eexp_ref, nw_ref, h0_ref, o_ref, hfin_ref, h_ref):
    step = pl.program_id(1)
    d_inner = N_HEADS * HEAD_DIM

    @pl.when(step == 0)
    def _():
        h_ref[...] = h0_ref[0]

    for j in reversed(range(xact_ref.shape[1] // CHUNK)):
        rows = slice(j * CHUNK, (j + 1) * CHUNK)
        exp_all = _dot(lhsb_ref[0, rows, :], eexp_ref[...])
        exp_w = exp_all[:, :d_inner]
        exp_e = exp_all[:, d_inner:]
        decay_row = exp_e[0:1, :]

        for g in range(SSM_GROUPS):
            gs, xs_g, _, c_g, bt_g = _group_operands(xact_ref, rows, g)
            y = y_ref[0, rows, gs].astype(F32) + _state_step(h_ref, c_g, bt_g, xs_g, exp_w, exp_e, decay_row, gs)
            z = z_ref[0, rows, gs].astype(F32)
            gated = y * (z * _sigmoid(z))
            ms = jnp.mean(gated * gated, axis=-1, keepdims=True)
            o_ref[0, rows, gs] = (gated * lax.rsqrt(ms + RMS_EPS) * nw_ref[:, gs]).astype(BF16)

    @pl.when(step == pl.num_programs(1) - 1)
    def _():
        hfin_ref[0] = h_ref[...]


def _ssd_b_call(xact, lhsb, ypart, z, lw, h0):
    b, t, dx = xact.shape
    d_inner = N_HEADS * HEAD_DIM
    tc = _scan_tile(t)
    ns = t // tc
    tok = lambda width: pl.BlockSpec((1, tc, width), lambda i, s: (i, ns - 1 - s, 0))
    state = pl.BlockSpec((1, D_STATE, d_inner), lambda i, s: (i, 0, 0))
    small = [lw["eexp"], lw["norm_w"]]
    return pl.pallas_call(
        _ssd_b_kernel,
        grid=(b, ns),
        in_specs=[tok(dx), tok(LANES), tok(d_inner), tok(d_inner)]
                 + [_full(a.shape) for a in small] + [state],
        out_specs=[tok(d_inner), state],
        out_shape=[jax.ShapeDtypeStruct((b, t, d_inner), BF16),
                   jax.ShapeDtypeStruct((b, D_STATE, d_inner), F32)],
        scratch_shapes=[pltpu.VMEM((D_STATE, d_inner), F32)],
        compiler_params=_params(("arbitrary", "arbitrary")),
        name="ssd_b",
    )(xact, lhsb, ypart, z, *small, h0)


def _tail_kernel(v_ref, g_ref, x_ref, gate1_ref, sh2_ref, sc2_ref, gate2_ref,
                 ccw_ref, ccb_ref, ccg_ref, ccbeta_ref, wc_ref, ws_ref, l1g_ref, l1b_ref,
                 w1_ref, w2_ref, l2g_ref, l2b_ref, o_ref, cpad_ref, cacc_ref, *, alpha, seg):
    tm, conv_dim = v_ref.shape[1], v_ref.shape[2]
    nseg = tm // seg
    stride = seg + 2 * CONV_HALO
    for cb in range(conv_dim // LANES):
        lanes = slice(cb * LANES, (cb + 1) * LANES)
        for s in range(nseg):
            base = s * stride
            cpad_ref[cb, base:base + CONV_HALO, :] = jnp.zeros((CONV_HALO, LANES), F32)
            cpad_ref[cb, base + CONV_HALO:base + CONV_HALO + seg, :] = v_ref[0, s * seg:(s + 1) * seg, lanes]
            cpad_ref[cb, base + CONV_HALO + seg:base + stride, :] = jnp.zeros((CONV_HALO, LANES), F32)
        for s in range(nseg):
            for rb in range(seg // CONV_ROWS):
                acc = jnp.zeros((CONV_ROWS, LANES), F32) + ccb_ref[:, lanes]
                for k in range(CONV_K):
                    r0 = s * stride + rb * CONV_ROWS + CONV_HALO - CONV_PAD + k
                    acc = acc + ccw_ref[k:k + 1, lanes] * cpad_ref[cb, r0:r0 + CONV_ROWS, :]
                r1 = s * seg + rb * CONV_ROWS
                cacc_ref[r1:r1 + CONV_ROWS, lanes] = acc
    y = _layer_norm(cacc_ref[...], ccg_ref[...], ccbeta_ref[...])
    cv = (y * _sigmoid(y)).astype(BF16)

    mix = _dot(cv, wc_ref[...]) + _dot(g_ref[0], ws_ref[...])
    x1 = _layer_norm(alpha * x_ref[0] + gate1_ref[0] * mix, l1g_ref[...], l1b_ref[...])

    hb = (x1 * (1.0 + sc2_ref[0]) + sh2_ref[0]).astype(BF16)
    acc = jnp.zeros(x1.shape, F32)
    for j in range(w1_ref.shape[1] // N_TILE):
        cols = slice(j * N_TILE, (j + 1) * N_TILE)
        t = jnp.maximum(_dot(hb, w1_ref[:, cols]), 0.0)
        acc = acc + _dot((t * t).astype(BF16), w2_ref[cols, :])
    o_ref[0] = _layer_norm(alpha * x1 + gate2_ref[0] * acc, l2g_ref[...], l2b_ref[...])


def _tail_call(v, gated, x, mods, lw, alpha, seg):
    b, t, d = x.shape
    conv_dim = v.shape[-1]
    tm = _token_tile(t)
    assert tm % seg == 0 and seg % CONV_ROWS == 0
    tok = lambda width: pl.BlockSpec((1, tm, width), lambda i, j: (i, j, 0))
    vec = pl.BlockSpec((1, 1, d), lambda i, j: (i, 0, 0))
    small = [lw["conv_w"], lw["conv_b"], lw["conv_ln_g"], lw["conv_ln_b"]]
    return pl.pallas_call(
        functools.partial(_tail_kernel, alpha=alpha, seg=seg),
        grid=(b, t // tm),
        in_specs=[tok(conv_dim), tok(gated.shape[-1]), tok(d), vec, vec, vec, vec]
                 + [_full(a.shape) for a in small]
                 + [_resident(lw["w_out_conv"].shape), _resident(lw["w_out_ssd"].shape), _full((1, d)), _full((1, d)),
                    _resident(lw["w1"].shape), _resident(lw["w2"].shape), _full((1, d)), _full((1, d))],
        out_specs=tok(d),
        out_shape=jax.ShapeDtypeStruct((b, t, d), F32),
        scratch_shapes=[pltpu.VMEM((conv_dim // LANES, (tm // seg) * (seg + 2 * CONV_HALO), LANES), F32),
                        pltpu.VMEM((tm, conv_dim), F32)],
        compiler_params=_params(("parallel", "parallel")),
        name="tail",
    )(v, gated, x, mods[2], mods[3], mods[4], mods[5], *small,
      lw["w_out_conv"], lw["w_out_ssd"], lw["ln1_g"], lw["ln1_b"], lw["w1"], lw["w2"], lw["ln2_g"], lw["ln2_b"])


def _mixer(x, mods, lw, h0_f, h0_b):
    v, z, xact, dt8 = _inproj_call(x, mods[0], mods[1], lw)
    ypart, lhsb, hfin_f = _ssd_a_call(xact, dt8, lw, h0_f)
    gated, hfin_b = _ssd_b_call(xact, lhsb, ypart, z, lw, h0_b)
    return v, gated, hfin_f, hfin_b


def kernel(x, c, ctx, c_ctx, w_mod, b_mod, w_in, conv_w, conv_b, conv_ln_g, conv_ln_b, ssm_conv_w,
           ssm_conv_b, dt_bias, a_log, d_skip, ssm_norm_w, w_out, ln1_g, ln1_b, w1, w2, ln2_g, ln2_b):
    depth = w_mod.shape[0]
    bsz, _, d = x.shape
    conv_dim = conv_w.shape[-1]
    d_xbc = ssm_conv_w.shape[-1]
    d_inner = ssm_norm_w.shape[-1]
    assert d_inner == N_HEADS * HEAD_DIM and d_xbc == d_inner + 2 * SSM_GROUPS * D_STATE
    assert x.shape[1] % (CHUNK * 4) == 0 and ctx.shape[1] % CHUNK == 0
    alpha = (2 * depth) ** 0.25
    n_main = 2 * conv_dim + d_inner + d_xbc

    mod_rows = 2 * SUBLANES
    c_all = jnp.zeros((mod_rows, d), F32).at[:bsz].set(c).at[bsz].set(c_ctx)
    eexp = _expand_matrix()
    tile_heads = lambda a: jnp.tile(a.astype(F32), DT_COPIES).reshape(1, LANES)
    per_lane = lambda a: jnp.repeat(a.astype(F32), HEAD_DIM).reshape(1, d_inner)

    x_l, x_c = x, ctx
    for i in range(depth):
        last = i == depth - 1
        lw = dict(
            conv_dim=conv_dim, d_inner=d_inner, d_xbc=d_xbc,
            w_main=w_in[i][:, :n_main].astype(BF16),
            w_dt=jnp.tile(w_in[i][:, n_main:], (1, DT_COPIES)).astype(BF16),
            conv_w=conv_w[i], conv_b=conv_b[i].reshape(1, conv_dim),
            conv_ln_g=conv_ln_g[i].reshape(1, conv_dim), conv_ln_b=conv_ln_b[i].reshape(1, conv_dim),
            ssm_cw=ssm_conv_w[i], ssm_cb=ssm_conv_b[i].reshape(1, d_xbc),
            bias_f=tile_heads(dt_bias[i, 0]), bias_b=tile_heads(dt_bias[i, 1]),
            alog_f=tile_heads(a_log[i, 0]), alog_b=tile_heads(a_log[i, 1]),
            dsk_f=per_lane(d_skip[i, 0]), dsk_b=per_lane(d_skip[i, 1]),
            eexp=eexp, norm_w=ssm_norm_w[i].reshape(1, d_inner),
            w_out_conv=w_out[i][:conv_dim].astype(BF16), w_out_ssd=w_out[i][conv_dim:].astype(BF16),
            ln1_g=ln1_g[i].reshape(1, d), ln1_b=ln1_b[i].reshape(1, d),
            w1=w1[i].astype(BF16), w2=w2[i].astype(BF16),
            ln2_g=ln2_g[i].reshape(1, d), ln2_b=ln2_b[i].reshape(1, d),
        )
        mod = _mod_call(c_all, w_mod, b_mod, i)
        mods_l = [mod[:bsz, j * d:(j + 1) * d].reshape(bsz, 1, d) for j in range(6)]
        mods_c = [jnp.broadcast_to(mod[bsz, j * d:(j + 1) * d].reshape(1, 1, d), (bsz, 1, d)) for j in range(6)]

        h_zero = jnp.zeros((bsz, D_STATE, d_inner), F32)
        v_c, gated_c, hc_f, hc_b = _mixer(x_c, mods_c, lw, h_zero, h_zero)
        v_l, gated_l, _, _ = _mixer(x_l, mods_l, lw, hc_f, hc_b)
        x_l = _tail_call(v_l, gated_l, x_l, mods_l, lw, alpha, GRID_W)
        if not last:
            x_c = _tail_call(v_c, gated_c, x_c, mods_c, lw, alpha, ctx.shape[1])
    return x_l
```

```python
import functools

import numpy as np
import jax
import jax.numpy as jnp
from jax import lax
from jax.experimental import pallas as pl
from jax.experimental.pallas import tpu as pltpu

F32 = jnp.float32
BF16 = jnp.bfloat16

GRID_W = 64
CONV_K = 31
CONV_PAD = CONV_K // 2
HEAD_DIM = 64
N_HEADS = 16
SSM_GROUPS = 4
HEADS_PER_GROUP = N_HEADS // SSM_GROUPS
D_STATE = 128
SSM_CONV_K = 5
SSM_PAD = SSM_CONV_K // 2
CHUNK = 128
LN_EPS = 1e-5
RMS_EPS = 1e-5
LOG2_E = 1.4426950408889634

LANES = 128
SUBLANES = 8
VMEM_LIMIT = 56 * 1024 * 1024

N_TILE = 512
CONV_ROWS = 64
CONV_HALO = 2 * SUBLANES
DT_COPIES = LANES // N_HEADS


def _token_tile(t):
    return 512 if t % 512 == 0 else 256


def _proj_tile(t):
    return 1024 if t % 1024 == 0 else 256


def _scan_tile(t):
    return 2048 if t % 2048 == 0 else 256


def _params(sem):
    return pltpu.CompilerParams(dimension_semantics=sem, vmem_limit_bytes=VMEM_LIMIT)


def _dot(a, b):
    return jnp.dot(a, b, preferred_element_type=F32)


def _sigmoid(x):
    return 1.0 / (1.0 + jnp.exp2(x * (-LOG2_E)))


def _softplus(x):
    return jnp.maximum(x, 0.0) + jnp.log1p(jnp.exp(-jnp.abs(x)))


def _layer_norm(u, g, b):
    mu = jnp.mean(u, axis=-1, keepdims=True)
    d = u - mu
    var = jnp.mean(d * d, axis=-1, keepdims=True)
    return d * lax.rsqrt(var + LN_EPS) * g + b


def _full(shape):
    nd = len(shape)
    return pl.BlockSpec(shape, lambda *_: (0,) * nd)


def _resident(shape):
    nd = len(shape)
    return pl.BlockSpec(shape, lambda *_: (0,) * nd, pipeline_mode=pl.Buffered(1))


def _mod_kernel(c_ref, w_ref, b_ref, o_ref):
    c = c_ref[...]
    a = c * _sigmoid(c)
    a_hi = a.astype(BF16)
    a_lo = (a - a_hi.astype(F32)).astype(BF16)
    w = w_ref[...]
    w_hi = w.astype(BF16)
    w_lo = (w - w_hi.astype(F32)).astype(BF16)
    o_ref[...] = _dot(a_hi, w_hi) + _dot(a_lo, w_hi) + _dot(a_hi, w_lo) + b_ref[...]


def _mod_call(c_all, w_mod, b_mod, layer):
    rows, d = c_all.shape
    depth, _, n = w_mod.shape
    tn = 1536
    return pl.pallas_call(
        _mod_kernel,
        grid=(n // tn,),
        in_specs=[_full((rows, d)),
                  pl.BlockSpec((None, d, tn), lambda j: (layer, 0, j)),
                  pl.BlockSpec((None, 1, tn), lambda j: (layer, 0, j))],
        out_specs=pl.BlockSpec((rows, tn), lambda j: (0, j)),
        out_shape=jax.ShapeDtypeStruct((rows, n), F32),
        compiler_params=_params(("parallel",)),
        name="mod",
    )(c_all, w_mod, b_mod.reshape(depth, 1, n))


def _inproj_kernel(x_ref, xp_ref, xn_ref, sh_ref, sc_ref, w_ref, wdt_ref, scw_ref, scb_ref,
                   v_ref, z_ref, xact_ref, dt_ref, spad_ref, *, conv_dim, d_inner, d_xbc):
    t = pl.program_id(1)
    nt = pl.num_programs(1)
    tm = x_ref.shape[1]
    scale = 1.0 + sc_ref[0]
    shift = sh_ref[0]
    h = x_ref[0] * scale + shift
    hb = h.astype(BF16)
    hb_ext = jnp.concatenate([xp_ref[0] * scale + shift, h, xn_ref[0] * scale + shift], axis=0).astype(BF16)

    a = _dot(hb, w_ref[:, 0:conv_dim])
    g = _dot(hb, w_ref[:, conv_dim:2 * conv_dim])
    v_ref[0] = a * _sigmoid(g)

    off = 2 * conv_dim
    for j in range(d_inner // N_TILE):
        z_ref[0, :, j * N_TILE:(j + 1) * N_TILE] = _dot(
            hb, w_ref[:, off + j * N_TILE:off + (j + 1) * N_TILE]).astype(BF16)
    off += d_inner

    tiles_per_chunk = N_TILE // LANES
    for j in range(d_xbc // N_TILE):
        res = _dot(hb_ext, w_ref[:, off + j * N_TILE:off + (j + 1) * N_TILE])
        for q in range(tiles_per_chunk):
            lt = j * tiles_per_chunk + q
            cols = slice(q * LANES, (q + 1) * LANES)
            lanes = slice(lt * LANES, (lt + 1) * LANES)
            spad_ref[lt, 0:SUBLANES, :] = jnp.where(t > 0, res[0:SUBLANES, cols], 0.0)
            spad_ref[lt, SUBLANES:SUBLANES + tm, :] = res[SUBLANES:SUBLANES + tm, cols]
            spad_ref[lt, SUBLANES + tm:2 * SUBLANES + tm, :] = jnp.where(
                t < nt - 1, res[SUBLANES + tm:2 * SUBLANES + tm, cols], 0.0)
            for rb in range(tm // CHUNK):
                acc = jnp.zeros((CHUNK, LANES), F32) + scb_ref[:, lanes]
                for k in range(SSM_CONV_K):
                    r0 = rb * CHUNK + SUBLANES - SSM_PAD + k
                    acc = acc + scw_ref[k:k + 1, lanes] * spad_ref[lt, r0:r0 + CHUNK, :]
                xact_ref[0, rb * CHUNK:(rb + 1) * CHUNK, lanes] = (acc * _sigmoid(acc)).astype(BF16)
    dt_ref[0] = _dot(hb, wdt_ref[...])


def _inproj_call(x, sh, sc, lw):
    b, t, d = x.shape
    conv_dim, d_inner, d_xbc = lw["conv_dim"], lw["d_inner"], lw["d_xbc"]
    tm = _proj_tile(t)
    rows8 = tm // SUBLANES
    last8 = t // SUBLANES - 1
    tok = lambda width: pl.BlockSpec((1, tm, width), lambda i, j: (i, j, 0))
    vec = pl.BlockSpec((1, 1, d), lambda i, j: (i, 0, 0))
    kern = functools.partial(_inproj_kernel, conv_dim=conv_dim, d_inner=d_inner, d_xbc=d_xbc)
    return pl.pallas_call(
        kern,
        grid=(b, t // tm),
        in_specs=[tok(d),
                  pl.BlockSpec((1, SUBLANES, d), lambda i, j: (i, jnp.maximum(j * rows8 - 1, 0), 0)),
                  pl.BlockSpec((1, SUBLANES, d), lambda i, j: (i, jnp.minimum((j + 1) * rows8, last8), 0)),
                  vec, vec, _resident(lw["w_main"].shape), _full(lw["w_dt"].shape),
                  _full(lw["ssm_cw"].shape), _full(lw["ssm_cb"].shape)],
        out_specs=[tok(conv_dim), tok(d_inner), tok(d_xbc), tok(LANES)],
        out_shape=[jax.ShapeDtypeStruct((b, t, conv_dim), F32),
                   jax.ShapeDtypeStruct((b, t, d_inner), BF16),
                   jax.ShapeDtypeStruct((b, t, d_xbc), BF16),
                   jax.ShapeDtypeStruct((b, t, LANES), F32)],
        scratch_shapes=[pltpu.VMEM((d_xbc // LANES, tm + 2 * SUBLANES, LANES), F32)],
        compiler_params=_params(("parallel", "parallel")),
        name="inproj",
    )(x, x, x, sh, sc, lw["w_main"], lw["w_dt"], lw["ssm_cw"], lw["ssm_cb"])


def _cumsum_rows(x, reverse):
    n = x.shape[0]
    row = lax.broadcasted_iota(jnp.int32, x.shape, 0)
    k = 1
    while k < n:
        if reverse:
            x = x + jnp.where(row < n - k, pltpu.roll(x, n - k, 0), 0.0)
        else:
            x = x + jnp.where(row >= k, pltpu.roll(x, k, 0), 0.0)
        k *= 2
    return x


def _expand_operand(w, e):
    lane = lax.broadcasted_iota(jnp.int32, w.shape, 1)
    x = jnp.where(lane < 2 * N_HEADS, w, e)
    x_hi = x.astype(BF16).astype(F32)
    x_lo = x - x_hi
    use_hi = ((lane // N_HEADS) % 2) == 0
    return jnp.where(lane < 4 * N_HEADS, jnp.where(use_hi, x_hi, x_lo), 0.0).astype(BF16)


def _expand_matrix():
    d_inner = N_HEADS * HEAD_DIM
    m = np.zeros((LANES, 2 * d_inner), np.float32)
    for j in range(4 * N_HEADS):
        h = j % N_HEADS
        base = (j // (2 * N_HEADS)) * d_inner
        m[j, base + h * HEAD_DIM:base + (h + 1) * HEAD_DIM] = 1.0
    return jnp.asarray(m, BF16)


def _state_step(h_ref, c_g, bt_g, xs_g, exp_w, exp_e, decay_row, gs):
    h_g = h_ref[:, gs]
    y_off = _dot(c_g, h_g.astype(BF16)) * exp_e[:, gs]
    xd = (xs_g.astype(F32) * exp_w[:, gs]).astype(BF16)
    h_ref[:, gs] = h_g * decay_row[:, gs] + _dot(bt_g, xd)
    return y_off


def _group_operands(xact_ref, rows, g):
    d_inner = N_HEADS * HEAD_DIM
    gw = HEADS_PER_GROUP * HEAD_DIM
    gs = slice(g * gw, (g + 1) * gw)
    b_g = xact_ref[0, rows, d_inner + g * D_STATE:d_inner + (g + 1) * D_STATE]
    c_g = xact_ref[0, rows, d_inner + (SSM_GROUPS + g) * D_STATE:d_inner + (SSM_GROUPS + g + 1) * D_STATE]
    return gs, xact_ref[0, rows, gs], b_g, c_g, b_g.astype(F32).T.astype(BF16)


def _ssd_a_kernel(xact_ref, dt_ref, bias_f_ref, bias_b_ref, alog_f_ref, alog_b_ref,
                  dsk_f_ref, dsk_b_ref, eexp_ref, h0_ref, y_ref, lhsb_ref, hfin_ref, h_ref):
    step = pl.program_id(1)
    d_inner = N_HEADS * HEAD_DIM

    @pl.when(step == 0)
    def _():
        h_ref[...] = h0_ref[0]

    row_i = lax.broadcasted_iota(jnp.int32, (CHUNK, CHUNK), 0)
    col_i = lax.broadcasted_iota(jnp.int32, (CHUNK, CHUNK), 1)
    src_before = col_i < row_i
    src_after = col_i > row_i
    left_half = col_i < HEAD_DIM
    blk = col_i // N_HEADS
    dsk = dsk_f_ref[...] + dsk_b_ref[...]
    a_f = -jnp.exp(alog_f_ref[...])
    a_b = -jnp.exp(alog_b_ref[...])

    for j in range(xact_ref.shape[1] // CHUNK):
        rows = slice(j * CHUNK, (j + 1) * CHUNK)
        raw = dt_ref[0, rows, :]
        dt_f = _softplus(raw + bias_f_ref[...])
        dt_b = _softplus(raw + bias_b_ref[...])
        cs_f = _cumsum_rows(dt_f * a_f, reverse=False)
        cs_b = _cumsum_rows(dt_b * a_b, reverse=True)
        cs2_f = cs_f * LOG2_E
        cs2_b = cs_b * LOG2_E
        stats = jnp.where(blk == 0, cs2_f,
                          jnp.where(blk == 1, cs2_b,
                                    jnp.where(blk == 2, cs2_f - jnp.log2(dt_f),
                                              jnp.where(blk == 3, cs2_b - jnp.log2(dt_b),
                                                        jnp.log2(dt_f + dt_b)))))
        stats_t = stats.T
        tot_f = cs_f[CHUNK - 1:CHUNK, :]
        exp_all = _dot(_expand_operand(dt_f * jnp.exp(tot_f - cs_f), jnp.exp(cs_f)), eexp_ref[...])
        exp_w = exp_all[:, :d_inner]
        exp_e = exp_all[:, d_inner:]
        decay_row = exp_e[CHUNK - 1:CHUNK, :]
        tot_b = cs_b[0:1, :]
        lhsb_ref[0, rows, :] = _expand_operand(dt_b * jnp.exp(tot_b - cs_b), jnp.exp(cs_b))

        for g in range(SSM_GROUPS):
            gs, xs_g, b_g, c_g, bt_g = _group_operands(xact_ref, rows, g)
            xs_f32 = xs_g.astype(F32)
            cbm = lax.dot_general(c_g, b_g, (((1,), (1,)), ((), ())), preferred_element_type=F32)
            y_off = _state_step(h_ref, c_g, bt_g, xs_f32, exp_w, exp_e, decay_row, gs)
            y_pairs = []
            for pr in range(HEADS_PER_GROUP // 2):
                mats = []
                for r in range(2):
                    h = g * HEADS_PER_GROUP + 2 * pr + r
                    arg = jnp.where(
                        src_before, stats[:, h:h + 1] - stats_t[2 * N_HEADS + h:2 * N_HEADS + h + 1, :],
                        jnp.where(src_after,
                                  stats[:, N_HEADS + h:N_HEADS + h + 1]
                                  - stats_t[3 * N_HEADS + h:3 * N_HEADS + h + 1, :],
                                  stats_t[4 * N_HEADS + h:4 * N_HEADS + h + 1, :]))
                    mats.append((cbm * jnp.exp2(arg)).astype(BF16))
                x_pair = xs_f32[:, pr * LANES:(pr + 1) * LANES]
                rhs = jnp.concatenate([jnp.where(left_half, x_pair, 0.0).astype(BF16),
                                       jnp.where(left_half, 0.0, x_pair).astype(BF16)], axis=0)
                y_pairs.append(_dot(jnp.concatenate(mats, axis=1), rhs))
            y = jnp.concatenate(y_pairs, axis=1) + y_off + dsk[:, gs] * xs_f32
            y_ref[0, rows, gs] = y.astype(BF16)

    @pl.when(step == pl.num_programs(1) - 1)
    def _():
        hfin_ref[0] = h_ref[...]


def _ssd_a_call(xact, dt8, lw, h0):
    b, t, dx = xact.shape
    d_inner = N_HEADS * HEAD_DIM
    tc = _scan_tile(t)
    tok = lambda width: pl.BlockSpec((1, tc, width), lambda i, s: (i, s, 0))
    state = pl.BlockSpec((1, D_STATE, d_inner), lambda i, s: (i, 0, 0))
    small = [lw["bias_f"], lw["bias_b"], lw["alog_f"], lw["alog_b"], lw["dsk_f"], lw["dsk_b"], lw["eexp"]]
    return pl.pallas_call(
        _ssd_a_kernel,
        grid=(b, t // tc),
        in_specs=[tok(dx), tok(LANES)] + [_full(a.shape) for a in small] + [state],
        out_specs=[tok(d_inner), tok(LANES), state],
        out_shape=[jax.ShapeDtypeStruct((b, t, d_inner), BF16),
                   jax.ShapeDtypeStruct((b, t, LANES), BF16),
                   jax.ShapeDtypeStruct((b, D_STATE, d_inner), F32)],
        scratch_shapes=[pltpu.VMEM((D_STATE, d_inner), F32)],
        compiler_params=_params(("arbitrary", "arbitrary")),
        name="ssd_a",
    )(xact, dt8, *small, h0)


def _ssd_b_kernel(xact_ref, lhsb_ref, y_ref, z_ref, eexp_ref, nw_ref, h0_ref, o_ref, hfin_ref, h_ref):
    step = pl.program_id(1)
    d_inner = N_HEADS * HEAD_DIM

    @pl.when(step == 0)
    def _():
        h_ref[...] = h0_ref[0]

    for j in reversed(range(xact_ref.shape[1] // CHUNK)):
        rows = slice(j * CHUNK, (j + 1) * CHUNK)
        exp_all = _dot(lhsb_ref[0, rows, :], eexp_ref[...])
        exp_w = exp_all[:, :d_inner]
        exp_e = exp_all[:, d_inner:]
        decay_row = exp_e[0:1, :]

        for g in range(SSM_GROUPS):
            gs, xs_g, _, c_g, bt_g = _group_operands(xact_ref, rows, g)
            y = y_ref[0, rows, gs].astype(F32) + _state_step(h_ref, c_g, bt_g, xs_g, exp_w, exp_e, decay_row, gs)
            z = z_ref[0, rows, gs].astype(F32)
            gated = y * (z * _sigmoid(z))
            ms = jnp.mean(gated * gated, axis=-1, keepdims=True)
            o_ref[0, rows, gs] = (gated * lax.rsqrt(ms + RMS_EPS) * nw_ref[:, gs]).astype(BF16)

    @pl.when(step == pl.num_programs(1) - 1)
    def _():
        hfin_ref[0] = h_ref[...]


def _ssd_b_call(xact, lhsb, ypart, z, lw, h0):
    b, t, dx = xact.shape
    d_inner = N_HEADS * HEAD_DIM
    tc = _scan_tile(t)
    ns = t // tc
    tok = lambda width: pl.BlockSpec((1, tc, width), lambda i, s: (i, ns - 1 - s, 0))
    state = pl.BlockSpec((1, D_STATE, d_inner), lambda i, s: (i, 0, 0))
    small = [lw["eexp"], lw["norm_w"]]
    return pl.pallas_call(
        _ssd_b_kernel,
        grid=(b, ns),
        in_specs=[tok(dx), tok(LANES), tok(d_inner), tok(d_inner)]
                 + [_full(a.shape) for a in small] + [state],
        out_specs=[tok(d_inner), state],
        out_shape=[jax.ShapeDtypeStruct((b, t, d_inner), BF16),
                   jax.ShapeDtypeStruct((b, D_STATE, d_inner), F32)],
        scratch_shapes=[pltpu.VMEM((D_STATE, d_inner), F32)],
        compiler_params=_params(("arbitrary", "arbitrary")),
        name="ssd_b",
    )(xact, lhsb, ypart, z, *small, h0)


def _tail_kernel(v_ref, g_ref, x_ref, gate1_ref, sh2_ref, sc2_ref, gate2_ref,
                 ccw_ref, ccb_ref, ccg_ref, ccbeta_ref, wc_ref, ws_ref, l1g_ref, l1b_ref,
                 w1_ref, w2_ref, l2g_ref, l2b_ref, o_ref, cpad_ref, cacc_ref, *, alpha, seg):
    tm, conv_dim = v_ref.shape[1], v_ref.shape[2]
    nseg = tm // seg
    stride = seg + 2 * CONV_HALO
    for cb in range(conv_dim // LANES):
        lanes = slice(cb * LANES, (cb + 1) * LANES)
        for s in range(nseg):
            base = s * stride
            cpad_ref[cb, base:base + CONV_HALO, :] = jnp.zeros((CONV_HALO, LANES), F32)
            cpad_ref[cb, base + CONV_HALO:base + CONV_HALO + seg, :] = v_ref[0, s * seg:(s + 1) * seg, lanes]
            cpad_ref[cb, base + CONV_HALO + seg:base + stride, :] = jnp.zeros((CONV_HALO, LANES), F32)
        for s in range(nseg):
            for rb in range(seg // CONV_ROWS):
                acc = jnp.zeros((CONV_ROWS, LANES), F32) + ccb_ref[:, lanes]
                for k in range(CONV_K):
                    r0 = s * stride + rb * CONV_ROWS + CONV_HALO - CONV_PAD + k
                    acc = acc + ccw_ref[k:k + 1, lanes] * cpad_ref[cb, r0:r0 + CONV_ROWS, :]
                r1 = s * seg + rb * CONV_ROWS
                cacc_ref[r1:r1 + CONV_ROWS, lanes] = acc
    y = _layer_norm(cacc_ref[...], ccg_ref[...], ccbeta_ref[...])
    cv = (y * _sigmoid(y)).astype(BF16)

    mix = _dot(cv, wc_ref[...]) + _dot(g_ref[0], ws_ref[...])
    x1 = _layer_norm(alpha * x_ref[0] + gate1_ref[0] * mix, l1g_ref[...], l1b_ref[...])

    hb = (x1 * (1.0 + sc2_ref[0]) + sh2_ref[0]).astype(BF16)
    acc = jnp.zeros(x1.shape, F32)
    for j in range(w1_ref.shape[1] // N_TILE):
        cols = slice(j * N_TILE, (j + 1) * N_TILE)
        t = jnp.maximum(_dot(hb, w1_ref[:, cols]), 0.0)
        acc = acc + _dot((t * t).astype(BF16), w2_ref[cols, :])
    o_ref[0] = _layer_norm(alpha * x1 + gate2_ref[0] * acc, l2g_ref[...], l2b_ref[...])


def _tail_call(v, gated, x, mods, lw, alpha, seg):
    b, t, d = x.shape
    conv_dim = v.shape[-1]
    tm = _token_tile(t)
    assert tm % seg == 0 and seg % CONV_ROWS == 0
    tok = lambda width: pl.BlockSpec((1, tm, width), lambda i, j: (i, j, 0))
    vec = pl.BlockSpec((1, 1, d), lambda i, j: (i, 0, 0))
    small = [lw["conv_w"], lw["conv_b"], lw["conv_ln_g"], lw["conv_ln_b"]]
    return pl.pallas_call(
        functools.partial(_tail_kernel, alpha=alpha, seg=seg),
        grid=(b, t // tm),
        in_specs=[tok(conv_dim), tok(gated.shape[-1]), tok(d), vec, vec, vec, vec]
                 + [_full(a.shape) for a in small]
                 + [_resident(lw["w_out_conv"].shape), _resident(lw["w_out_ssd"].shape), _full((1, d)), _full((1, d)),
                    _resident(lw["w1"].shape), _resident(lw["w2"].shape), _full((1, d)), _full((1, d))],
        out_specs=tok(d),
        out_shape=jax.ShapeDtypeStruct((b, t, d), F32),
        scratch_shapes=[pltpu.VMEM((conv_dim // LANES, (tm // seg) * (seg + 2 * CONV_HALO), LANES), F32),
                        pltpu.VMEM((tm, conv_dim), F32)],
        compiler_params=_params(("parallel", "parallel")),
        name="tail",
    )(v, gated, x, mods[2], mods[3], mods[4], mods[5], *small,
      lw["w_out_conv"], lw["w_out_ssd"], lw["ln1_g"], lw["ln1_b"], lw["w1"], lw["w2"], lw["ln2_g"], lw["ln2_b"])


def _mixer(x, mods, lw, h0_f, h0_b):
    v, z, xact, dt8 = _inproj_call(x, mods[0], mods[1], lw)
    ypart, lhsb, hfin_f = _ssd_a_call(xact, dt8, lw, h0_f)
    gated, hfin_b = _ssd_b_call(xact, lhsb, ypart, z, lw, h0_b)
    return v, gated, hfin_f, hfin_b


def kernel(x, c, ctx, c_ctx, w_mod, b_mod, w_in, conv_w, conv_b, conv_ln_g, conv_ln_b, ssm_conv_w,
           ssm_conv_b, dt_bias, a_log, d_skip, ssm_norm_w, w_out, ln1_g, ln1_b, w1, w2, ln2_g, ln2_b):
    depth = w_mod.shape[0]
    bsz, _, d = x.shape
    conv_dim = conv_w.shape[-1]
    d_xbc = ssm_conv_w.shape[-1]
    d_inner = ssm_norm_w.shape[-1]
    assert d_inner == N_HEADS * HEAD_DIM and d_xbc == d_inner + 2 * SSM_GROUPS * D_STATE
    assert x.shape[1] % (CHUNK * 4) == 0 and ctx.shape[1] % CHUNK == 0
    alpha = (2 * depth) ** 0.25
    n_main = 2 * conv_dim + d_inner + d_xbc

    mod_rows = 2 * SUBLANES
    c_all = jnp.zeros((mod_rows, d), F32).at[:bsz].set(c).at[bsz].set(c_ctx)
    eexp = _expand_matrix()
    tile_heads = lambda a: jnp.tile(a.astype(F32), DT_COPIES).reshape(1, LANES)
    per_lane = lambda a: jnp.repeat(a.astype(F32), HEAD_DIM).reshape(1, d_inner)

    x_l, x_c = x, ctx
    for i in range(depth):
        last = i == depth - 1
        lw = dict(
            conv_dim=conv_dim, d_inner=d_inner, d_xbc=d_xbc,
            w_main=w_in[i][:, :n_main].astype(BF16),
            w_dt=jnp.tile(w_in[i][:, n_main:], (1, DT_COPIES)).astype(BF16),
            conv_w=conv_w[i], conv_b=conv_b[i].reshape(1, conv_dim),
            conv_ln_g=conv_ln_g[i].reshape(1, conv_dim), conv_ln_b=conv_ln_b[i].reshape(1, conv_dim),
            ssm_cw=ssm_conv_w[i], ssm_cb=ssm_conv_b[i].reshape(1, d_xbc),
            bias_f=tile_heads(dt_bias[i, 0]), bias_b=tile_heads(dt_bias[i, 1]),
            alog_f=tile_heads(a_log[i, 0]), alog_b=tile_heads(a_log[i, 1]),
            dsk_f=per_lane(d_skip[i, 0]), dsk_b=per_lane(d_skip[i, 1]),
            eexp=eexp, norm_w=ssm_norm_w[i].reshape(1, d_inner),
            w_out_conv=w_out[i][:conv_dim].astype(BF16), w_out_ssd=w_out[i][conv_dim:].astype(BF16),
            ln1_g=ln1_g[i].reshape(1, d), ln1_b=ln1_b[i].reshape(1, d),
            w1=w1[i].astype(BF16), w2=w2[i].astype(BF16),
            ln2_g=ln2_g[i].reshape(1, d), ln2_b=ln2_b[i].reshape(1, d),
        )
        mod = _mod_call(c_all, w_mod, b_mod, i)
        mods_l = [mod[:bsz, j * d:(j + 1) * d].reshape(bsz, 1, d) for j in range(6)]
        mods_c = [jnp.broadcast_to(mod[bsz, j * d:(j + 1) * d].reshape(1, 1, d), (bsz, 1, d)) for j in range(6)]

        h_zero = jnp.zeros((bsz, D_STATE, d_inner), F32)
        v_c, gated_c, hc_f, hc_b = _mixer(x_c, mods_c, lw, h_zero, h_zero)
        v_l, gated_l, _, _ = _mixer(x_l, mods_l, lw, hc_f, hc_b)
        x_l = _tail_call(v_l, gated_l, x_l, mods_l, lw, alpha, GRID_W)
        if not last:
            x_c = _tail_call(v_c, gated_c, x_c, mods_c, lw, alpha, ctx.shape[1])
    return x_l
```

```python
import functools

import numpy as np
import jax
import jax.numpy as jnp
from jax import lax
from jax.experimental import pallas as pl
from jax.experimental.pallas import tpu as pltpu

F32 = jnp.float32
BF16 = jnp.bfloat16

GRID_W = 64
CONV_K = 31
CONV_PAD = CONV_K // 2
HEAD_DIM = 64
N_HEADS = 16
SSM_GROUPS = 4
HEADS_PER_GROUP = N_HEADS // SSM_GROUPS
D_STATE = 128
SSM_CONV_K = 5
SSM_PAD = SSM_CONV_K // 2
CHUNK = 128
LN_EPS = 1e-5
RMS_EPS = 1e-5
LOG2_E = 1.4426950408889634

LANES = 128
SUBLANES = 8
VMEM_LIMIT = 56 * 1024 * 1024

N_TILE = 512
CONV_ROWS = 64
CONV_HALO = 2 * SUBLANES
DT_COPIES = LANES // N_HEADS


def _token_tile(t):
    return 512 if t % 512 == 0 else 256


def _proj_tile(t):
    return 1024 if t % 1024 == 0 else 256


def _scan_tile(t):
    return 2048 if t % 2048 == 0 else 256


def _params(sem):
    return pltpu.CompilerParams(dimension_semantics=sem, vmem_limit_bytes=VMEM_LIMIT)


def _dot(a, b):
    return jnp.dot(a, b, preferred_element_type=F32)


def _sigmoid(x):
    return 1.0 / (1.0 + jnp.exp2(x * (-LOG2_E)))


def _softplus(x):
    return jnp.maximum(x, 0.0) + jnp.log1p(jnp.exp(-jnp.abs(x)))


def _layer_norm(u, g, b):
    mu = jnp.mean(u, axis=-1, keepdims=True)
    d = u - mu
    var = jnp.mean(d * d, axis=-1, keepdims=True)
    return d * lax.rsqrt(var + LN_EPS) * g + b


def _full(shape):
    nd = len(shape)
    return pl.BlockSpec(shape, lambda *_: (0,) * nd)


def _mod_spec(mods, k, d):
    _, shared_row = mods
    if shared_row is None:
        return pl.BlockSpec((1, 1, d), lambda i, j: (i, 0, k))
    return pl.BlockSpec((1, 1, d), lambda i, j: (shared_row, 0, k))


def _resident(shape):
    nd = len(shape)
    return pl.BlockSpec(shape, lambda *_: (0,) * nd, pipeline_mode=pl.Buffered(1))


def _mod_kernel(c_ref, w_ref, b_ref, o_ref):
    c = c_ref[...]
    a = c * _sigmoid(c)
    a_hi = a.astype(BF16)
    a_lo = (a - a_hi.astype(F32)).astype(BF16)
    w = w_ref[...]
    w_hi = w.astype(BF16)
    w_lo = (w - w_hi.astype(F32)).astype(BF16)
    o_ref[...] = _dot(a_hi, w_hi) + _dot(a_lo, w_hi) + _dot(a_hi, w_lo) + b_ref[...]


def _mod_call(c_all, w_mod, b_mod, layer):
    rows, d = c_all.shape
    depth, _, n = w_mod.shape
    tn = 1536
    return pl.pallas_call(
        _mod_kernel,
        grid=(n // tn,),
        in_specs=[_full((rows, d)),
                  pl.BlockSpec((None, d, tn), lambda j: (layer, 0, j)),
                  pl.BlockSpec((None, 1, tn), lambda j: (layer, 0, j))],
        out_specs=pl.BlockSpec((rows, tn), lambda j: (0, j)),
        out_shape=jax.ShapeDtypeStruct((rows, n), F32),
        compiler_params=_params(("parallel",)),
        name="mod",
    )(c_all, w_mod, b_mod.reshape(depth, 1, n))


def _inproj_kernel(x_ref, xp_ref, xn_ref, sh_ref, sc_ref, w_ref, wdt_ref, scw_ref, scb_ref,
                   v_ref, z_ref, xact_ref, dt_ref, spad_ref, *, conv_dim, d_inner, d_xbc):
    t = pl.program_id(1)
    nt = pl.num_programs(1)
    tm = x_ref.shape[1]
    scale = 1.0 + sc_ref[0]
    shift = sh_ref[0]
    h = x_ref[0] * scale + shift
    hb = h.astype(BF16)
    hb_ext = jnp.concatenate([xp_ref[0] * scale + shift, h, xn_ref[0] * scale + shift], axis=0).astype(BF16)

    a = _dot(hb, w_ref[:, 0:conv_dim])
    g = _dot(hb, w_ref[:, conv_dim:2 * conv_dim])
    v_ref[0] = a * _sigmoid(g)

    off = 2 * conv_dim
    for j in range(d_inner // N_TILE):
        z_ref[0, :, j * N_TILE:(j + 1) * N_TILE] = _dot(
            hb, w_ref[:, off + j * N_TILE:off + (j + 1) * N_TILE]).astype(BF16)
    off += d_inner

    tiles_per_chunk = N_TILE // LANES
    for j in range(d_xbc // N_TILE):
        res = _dot(hb_ext, w_ref[:, off + j * N_TILE:off + (j + 1) * N_TILE])
        for q in range(tiles_per_chunk):
            lt = j * tiles_per_chunk + q
            cols = slice(q * LANES, (q + 1) * LANES)
            lanes = slice(lt * LANES, (lt + 1) * LANES)
            spad_ref[lt, 0:SUBLANES, :] = jnp.where(t > 0, res[0:SUBLANES, cols], 0.0)
            spad_ref[lt, SUBLANES:SUBLANES + tm, :] = res[SUBLANES:SUBLANES + tm, cols]
            spad_ref[lt, SUBLANES + tm:2 * SUBLANES + tm, :] = jnp.where(
                t < nt - 1, res[SUBLANES + tm:2 * SUBLANES + tm, cols], 0.0)
            for rb in range(tm // CHUNK):
                acc = jnp.zeros((CHUNK, LANES), F32) + scb_ref[:, lanes]
                for k in range(SSM_CONV_K):
                    r0 = rb * CHUNK + SUBLANES - SSM_PAD + k
                    acc = acc + scw_ref[k:k + 1, lanes] * spad_ref[lt, r0:r0 + CHUNK, :]
                xact_ref[0, rb * CHUNK:(rb + 1) * CHUNK, lanes] = (acc * _sigmoid(acc)).astype(BF16)
    dt_ref[0] = _dot(hb, wdt_ref[...])


def _inproj_call(x, mods, lw):
    b, t, d = x.shape
    conv_dim, d_inner, d_xbc = lw["conv_dim"], lw["d_inner"], lw["d_xbc"]
    tm = _proj_tile(t)
    rows8 = tm // SUBLANES
    last8 = t // SUBLANES - 1
    tok = lambda width: pl.BlockSpec((1, tm, width), lambda i, j: (i, j, 0))
    kern = functools.partial(_inproj_kernel, conv_dim=conv_dim, d_inner=d_inner, d_xbc=d_xbc)
    return pl.pallas_call(
        kern,
        grid=(b, t // tm),
        in_specs=[tok(d),
                  pl.BlockSpec((1, SUBLANES, d), lambda i, j: (i, jnp.maximum(j * rows8 - 1, 0), 0)),
                  pl.BlockSpec((1, SUBLANES, d), lambda i, j: (i, jnp.minimum((j + 1) * rows8, last8), 0)),
                  _mod_spec(mods, 0, d), _mod_spec(mods, 1, d),
                  _resident(lw["w_main"].shape), _full(lw["w_dt"].shape),
                  _full(lw["ssm_cw"].shape), _full(lw["ssm_cb"].shape)],
        out_specs=[tok(conv_dim), tok(d_inner), tok(d_xbc), tok(LANES)],
        out_shape=[jax.ShapeDtypeStruct((b, t, conv_dim), F32),
                   jax.ShapeDtypeStruct((b, t, d_inner), BF16),
                   jax.ShapeDtypeStruct((b, t, d_xbc), BF16),
                   jax.ShapeDtypeStruct((b, t, LANES), F32)],
        scratch_shapes=[pltpu.VMEM((d_xbc // LANES, tm + 2 * SUBLANES, LANES), F32)],
        compiler_params=_params(("parallel", "parallel")),
        name="inproj",
    )(x, x, x, mods[0], mods[0], lw["w_main"], lw["w_dt"], lw["ssm_cw"], lw["ssm_cb"])


def _cumsum_rows(x, reverse):
    n = x.shape[0]
    row = lax.broadcasted_iota(jnp.int32, x.shape, 0)
    k = 1
    while k < n:
        if reverse:
            x = x + jnp.where(row < n - k, pltpu.roll(x, n - k, 0), 0.0)
        else:
            x = x + jnp.where(row >= k, pltpu.roll(x, k, 0), 0.0)
        k *= 2
    return x


def _expand_operand(w, e):
    lane = lax.broadcasted_iota(jnp.int32, w.shape, 1)
    x = jnp.where(lane < 2 * N_HEADS, w, e)
    x_hi = x.astype(BF16).astype(F32)
    x_lo = x - x_hi
    use_hi = ((lane // N_HEADS) % 2) == 0
    return jnp.where(lane < 4 * N_HEADS, jnp.where(use_hi, x_hi, x_lo), 0.0).astype(BF16)


def _expand_matrix():
    d_inner = N_HEADS * HEAD_DIM
    m = np.zeros((LANES, 2 * d_inner), np.float32)
    for j in range(4 * N_HEADS):
        h = j % N_HEADS
        base = (j // (2 * N_HEADS)) * d_inner
        m[j, base + h * HEAD_DIM:base + (h + 1) * HEAD_DIM] = 1.0
    return jnp.asarray(m, BF16)


def _state_step(h_ref, c_g, bt_g, xs_g, exp_w, exp_e, decay_row, gs):
    h_g = h_ref[:, gs]
    y_off = _dot(c_g, h_g.astype(BF16)) * exp_e[:, gs]
    xd = (xs_g.astype(F32) * exp_w[:, gs]).astype(BF16)
    h_ref[:, gs] = h_g * decay_row[:, gs] + _dot(bt_g, xd)
    return y_off


def _group_operands(xact_ref, rows, g):
    d_inner = N_HEADS * HEAD_DIM
    gw = HEADS_PER_GROUP * HEAD_DIM
    gs = slice(g * gw, (g + 1) * gw)
    b_g = xact_ref[0, rows, d_inner + g * D_STATE:d_inner + (g + 1) * D_STATE]
    c_g = xact_ref[0, rows, d_inner + (SSM_GROUPS + g) * D_STATE:d_inner + (SSM_GROUPS + g + 1) * D_STATE]
    return gs, xact_ref[0, rows, gs], b_g, c_g, b_g.astype(F32).T.astype(BF16)


def _ssd_a_kernel(xact_ref, dt_ref, bias_f_ref, bias_b_ref, alog_f_ref, alog_b_ref,
                  dsk_f_ref, dsk_b_ref, eexp_ref, h0_ref, y_ref, lhsb_ref, hfin_ref, h_ref):
    step = pl.program_id(1)
    d_inner = N_HEADS * HEAD_DIM

    @pl.when(step == 0)
    def _():
        h_ref[...] = h0_ref[0]

    row_i = lax.broadcasted_iota(jnp.int32, (CHUNK, CHUNK), 0)
    col_i = lax.broadcasted_iota(jnp.int32, (CHUNK, CHUNK), 1)
    src_before = col_i < row_i
    src_after = col_i > row_i
    left_half = col_i < HEAD_DIM
    blk = col_i // N_HEADS
    dsk = dsk_f_ref[...] + dsk_b_ref[...]
    a_f = -jnp.exp(alog_f_ref[...])
    a_b = -jnp.exp(alog_b_ref[...])

    for j in range(xact_ref.shape[1] // CHUNK):
        rows = slice(j * CHUNK, (j + 1) * CHUNK)
        raw = dt_ref[0, rows, :]
        dt_f = _softplus(raw + bias_f_ref[...])
        dt_b = _softplus(raw + bias_b_ref[...])
        cs_f = _cumsum_rows(dt_f * a_f, reverse=False)
        cs_b = _cumsum_rows(dt_b * a_b, reverse=True)
        cs2_f = cs_f * LOG2_E
        cs2_b = cs_b * LOG2_E
        stats = jnp.where(blk == 0, cs2_f,
                          jnp.where(blk == 1, cs2_b,
                                    jnp.where(blk == 2, cs2_f - jnp.log2(dt_f),
                                              jnp.where(blk == 3, cs2_b - jnp.log2(dt_b),
                                                        jnp.log2(dt_f + dt_b)))))
        stats_t = stats.T
        tot_f = cs_f[CHUNK - 1:CHUNK, :]
        exp_all = _dot(_expand_operand(dt_f * jnp.exp(tot_f - cs_f), jnp.exp(cs_f)), eexp_ref[...])
        exp_w = exp_all[:, :d_inner]
        exp_e = exp_all[:, d_inner:]
        decay_row = exp_e[CHUNK - 1:CHUNK, :]
        tot_b = cs_b[0:1, :]
        lhsb_ref[0, rows, :] = _expand_operand(dt_b * jnp.exp(tot_b - cs_b), jnp.exp(cs_b))

        for g in range(SSM_GROUPS):
            gs, xs_g, b_g, c_g, bt_g = _group_operands(xact_ref, rows, g)
            xs_f32 = xs_g.astype(F32)
            cbm = lax.dot_general(c_g, b_g, (((1,), (1,)), ((), ())), preferred_element_type=F32)
            y_off = _state_step(h_ref, c_g, bt_g, xs_f32, exp_w, exp_e, decay_row, gs)
            y_pairs = []
            for pr in range(HEADS_PER_GROUP // 2):
                mats = []
                for r in range(2):
                    h = g * HEADS_PER_GROUP + 2 * pr + r
                    arg = jnp.where(
                        src_before, stats[:, h:h + 1] - stats_t[2 * N_HEADS + h:2 * N_HEADS + h + 1, :],
                        jnp.where(src_after,
                                  stats[:, N_HEADS + h:N_HEADS + h + 1]
                                  - stats_t[3 * N_HEADS + h:3 * N_HEADS + h + 1, :],
                                  stats_t[4 * N_HEADS + h:4 * N_HEADS + h + 1, :]))
                    mats.append((cbm * jnp.exp2(arg)).astype(BF16))
                x_pair = xs_f32[:, pr * LANES:(pr + 1) * LANES]
                rhs = jnp.concatenate([jnp.where(left_half, x_pair, 0.0).astype(BF16),
                                       jnp.where(left_half, 0.0, x_pair).astype(BF16)], axis=0)
                y_pairs.append(_dot(jnp.concatenate(mats, axis=1), rhs))
            y = jnp.concatenate(y_pairs, axis=1) + y_off + dsk[:, gs] * xs_f32
            y_ref[0, rows, gs] = y.astype(BF16)

    @pl.when(step == pl.num_programs(1) - 1)
    def _():
        hfin_ref[0] = h_ref[...]


def _ssd_a_call(xact, dt8, lw, h0):
    b, t, dx = xact.shape
    d_inner = N_HEADS * HEAD_DIM
    tc = _scan_tile(t)
    tok = lambda width: pl.BlockSpec((1, tc, width), lambda i, s: (i, s, 0))
    state = pl.BlockSpec((1, D_STATE, d_inner), lambda i, s: (i, 0, 0))
    small = [lw["bias_f"], lw["bias_b"], lw["alog_f"], lw["alog_b"], lw["dsk_f"], lw["dsk_b"], lw["eexp"]]
    return pl.pallas_call(
        _ssd_a_kernel,
        grid=(b, t // tc),
        in_specs=[tok(dx), tok(LANES)] + [_full(a.shape) for a in small] + [state],
        out_specs=[tok(d_inner), tok(LANES), state],
        out_shape=[jax.ShapeDtypeStruct((b, t, d_inner), BF16),
                   jax.ShapeDtypeStruct((b, t, LANES), BF16),
                   jax.ShapeDtypeStruct((b, D_STATE, d_inner), F32)],
        scratch_shapes=[pltpu.VMEM((D_STATE, d_inner), F32)],
        compiler_params=_params(("arbitrary", "arbitrary")),
        name="ssd_a",
    )(xact, dt8, *small, h0)


def _ssd_b_kernel(xact_ref, lhsb_ref, y_ref, z_ref, eexp_ref, nw_ref, h0_ref, o_ref, hfin_ref, h_ref):
    step = pl.program_id(1)
    d_inner = N_HEADS * HEAD_DIM

    @pl.when(step == 0)
    def _():
        h_ref[...] = h0_ref[0]

    for j in reversed(range(xact_ref.shape[1] // CHUNK)):
        rows = slice(j * CHUNK, (j + 1) * CHUNK)
        exp_all = _dot(lhsb_ref[0, rows, :], eexp_ref[...])
        exp_w = exp_all[:, :d_inner]
        exp_e = exp_all[:, d_inner:]
        decay_row = exp_e[0:1, :]

        for g in range(SSM_GROUPS):
            gs, xs_g, _, c_g, bt_g = _group_operands(xact_ref, rows, g)
            y = y_ref[0, rows, gs].astype(F32) + _state_step(h_ref, c_g, bt_g, xs_g, exp_w, exp_e, decay_row, gs)
            z = z_ref[0, rows, gs].astype(F32)
            gated = y * (z * _sigmoid(z))
            ms = jnp.mean(gated * gated, axis=-1, keepdims=True)
            o_ref[0, rows, gs] = (gated * lax.rsqrt(ms + RMS_EPS) * nw_ref[:, gs]).astype(BF16)

    @pl.when(step == pl.num_programs(1) - 1)
    def _():
        hfin_ref[0] = h_ref[...]


def _ssd_b_call(xact, lhsb, ypart, z, lw, h0):
    b, t, dx = xact.shape
    d_inner = N_HEADS * HEAD_DIM
    tc = _scan_tile(t)
    ns = t // tc
    tok = lambda width: pl.BlockSpec((1, tc, width), lambda i, s: (i, ns - 1 - s, 0))
    state = pl.BlockSpec((1, D_STATE, d_inner), lambda i, s: (i, 0, 0))
    small = [lw["eexp"], lw["norm_w"]]
    return pl.pallas_call(
        _ssd_b_kernel,
        grid=(b, ns),
        in_specs=[tok(dx), tok(LANES), tok(d_inner), tok(d_inner)]
                 + [_full(a.shape) for a in small] + [state],
        out_specs=[tok(d_inner), state],
        out_shape=[jax.ShapeDtypeStruct((b, t, d_inner), BF16),
                   jax.ShapeDtypeStruct((b, D_STATE, d_inner), F32)],
        scratch_shapes=[pltpu.VMEM((D_STATE, d_inner), F32)],
        compiler_params=_params(("arbitrary", "arbitrary")),
        name="ssd_b",
    )(xact, lhsb, ypart, z, *small, h0)


def _tail_kernel(v_ref, g_ref, x_ref, gate1_ref, sh2_ref, sc2_ref, gate2_ref,
                 ccw_ref, ccb_ref, ccg_ref, ccbeta_ref, wc_ref, ws_ref, l1g_ref, l1b_ref,
                 w1_ref, w2_ref, l2g_ref, l2b_ref, o_ref, cpad_ref, cacc_ref, *, alpha, seg):
    tm, conv_dim = v_ref.shape[1], v_ref.shape[2]
    nseg = tm // seg
    stride = seg + 2 * CONV_HALO
    for cb in range(conv_dim // LANES):
        lanes = slice(cb * LANES, (cb + 1) * LANES)
        for s in range(nseg):
            base = s * stride
            cpad_ref[cb, base:base + CONV_HALO, :] = jnp.zeros((CONV_HALO, LANES), F32)
            cpad_ref[cb, base + CONV_HALO:base + CONV_HALO + seg, :] = v_ref[0, s * seg:(s + 1) * seg, lanes]
            cpad_ref[cb, base + CONV_HALO + seg:base + stride, :] = jnp.zeros((CONV_HALO, LANES), F32)
        for s in range(nseg):
            for rb in range(seg // CONV_ROWS):
                acc = jnp.zeros((CONV_ROWS, LANES), F32) + ccb_ref[:, lanes]
                for k in range(CONV_K):
                    r0 = s * stride + rb * CONV_ROWS + CONV_HALO - CONV_PAD + k
                    acc = acc + ccw_ref[k:k + 1, lanes] * cpad_ref[cb, r0:r0 + CONV_ROWS, :]
                r1 = s * seg + rb * CONV_ROWS
                cacc_ref[r1:r1 + CONV_ROWS, lanes] = acc
    y = _layer_norm(cacc_ref[...], ccg_ref[...], ccbeta_ref[...])
    cv = (y * _sigmoid(y)).astype(BF16)

    mix = _dot(cv, wc_ref[...]) + _dot(g_ref[0], ws_ref[...])
    x1 = _layer_norm(alpha * x_ref[0] + gate1_ref[0] * mix, l1g_ref[...], l1b_ref[...])

    hb = (x1 * (1.0 + sc2_ref[0]) + sh2_ref[0]).astype(BF16)
    acc = jnp.zeros(x1.shape, F32)
    for j in range(w1_ref.shape[1] // N_TILE):
        cols = slice(j * N_TILE, (j + 1) * N_TILE)
        t = jnp.maximum(_dot(hb, w1_ref[:, cols]), 0.0)
        acc = acc + _dot((t * t).astype(BF16), w2_ref[cols, :])
    o_ref[0] = _layer_norm(alpha * x1 + gate2_ref[0] * acc, l2g_ref[...], l2b_ref[...])


def _tail_call(v, gated, x, mods, lw, alpha, seg):
    b, t, d = x.shape
    conv_dim = v.shape[-1]
    tm = _token_tile(t)
    assert tm % seg == 0 and seg % CONV_ROWS == 0
    tok = lambda width: pl.BlockSpec((1, tm, width), lambda i, j: (i, j, 0))
    small = [lw["conv_w"], lw["conv_b"], lw["conv_ln_g"], lw["conv_ln_b"]]
    return pl.pallas_call(
        functools.partial(_tail_kernel, alpha=alpha, seg=seg),
        grid=(b, t // tm),
        in_specs=[tok(conv_dim), tok(gated.shape[-1]), tok(d)] + [_mod_spec(mods, k, d) for k in (2, 3, 4, 5)]
                 + [_full(a.shape) for a in small]
                 + [_resident(lw["w_out_conv"].shape), _resident(lw["w_out_ssd"].shape), _full((1, d)), _full((1, d)),
                    _resident(lw["w1"].shape), _resident(lw["w2"].shape), _full((1, d)), _full((1, d))],
        out_specs=tok(d),
        out_shape=jax.ShapeDtypeStruct((b, t, d), F32),
        scratch_shapes=[pltpu.VMEM((conv_dim // LANES, (tm // seg) * (seg + 2 * CONV_HALO), LANES), F32),
                        pltpu.VMEM((tm, conv_dim), F32)],
        compiler_params=_params(("parallel", "parallel")),
        name="tail",
    )(v, gated, x, mods[0], mods[0], mods[0], mods[0], *small,
      lw["w_out_conv"], lw["w_out_ssd"], lw["ln1_g"], lw["ln1_b"], lw["w1"], lw["w2"], lw["ln2_g"], lw["ln2_b"])


def _mixer(x, mods, lw, h0_f, h0_b):
    v, z, xact, dt8 = _inproj_call(x, mods, lw)
    ypart, lhsb, hfin_f = _ssd_a_call(xact, dt8, lw, h0_f)
    gated, hfin_b = _ssd_b_call(xact, lhsb, ypart, z, lw, h0_b)
    return v, gated, hfin_f, hfin_b


def kernel(x, c, ctx, c_ctx, w_mod, b_mod, w_in, conv_w, conv_b, conv_ln_g, conv_ln_b, ssm_conv_w,
           ssm_conv_b, dt_bias, a_log, d_skip, ssm_norm_w, w_out, ln1_g, ln1_b, w1, w2, ln2_g, ln2_b):
    depth = w_mod.shape[0]
    bsz, _, d = x.shape
    conv_dim = conv_w.shape[-1]
    d_xbc = ssm_conv_w.shape[-1]
    d_inner = ssm_norm_w.shape[-1]
    assert d_inner == N_HEADS * HEAD_DIM and d_xbc == d_inner + 2 * SSM_GROUPS * D_STATE
    assert x.shape[1] % (CHUNK * 4) == 0 and ctx.shape[1] % CHUNK == 0
    alpha = (2 * depth) ** 0.25
    n_main = 2 * conv_dim + d_inner + d_xbc

    mod_rows = 2 * SUBLANES
    c_all = jnp.zeros((mod_rows, d), F32).at[:bsz].set(c).at[bsz].set(c_ctx)
    eexp = _expand_matrix()
    tile_heads = lambda a: jnp.tile(a.astype(F32), DT_COPIES).reshape(1, LANES)
    per_lane = lambda a: jnp.repeat(a.astype(F32), HEAD_DIM).reshape(1, d_inner)

    x_l, x_c = x, ctx
    for i in range(depth):
        last = i == depth - 1
        lw = dict(
            conv_dim=conv_dim, d_inner=d_inner, d_xbc=d_xbc,
            w_main=w_in[i][:, :n_main].astype(BF16),
            w_dt=jnp.tile(w_in[i][:, n_main:], (1, DT_COPIES)).astype(BF16),
            conv_w=conv_w[i], conv_b=conv_b[i].reshape(1, conv_dim),
            conv_ln_g=conv_ln_g[i].reshape(1, conv_dim), conv_ln_b=conv_ln_b[i].reshape(1, conv_dim),
            ssm_cw=ssm_conv_w[i], ssm_cb=ssm_conv_b[i].reshape(1, d_xbc),
            bias_f=tile_heads(dt_bias[i, 0]), bias_b=tile_heads(dt_bias[i, 1]),
            alog_f=tile_heads(a_log[i, 0]), alog_b=tile_heads(a_log[i, 1]),
            dsk_f=per_lane(d_skip[i, 0]), dsk_b=per_lane(d_skip[i, 1]),
            eexp=eexp, norm_w=ssm_norm_w[i].reshape(1, d_inner),
            w_out_conv=w_out[i][:conv_dim].astype(BF16), w_out_ssd=w_out[i][conv_dim:].astype(BF16),
            ln1_g=ln1_g[i].reshape(1, d), ln1_b=ln1_b[i].reshape(1, d),
            w1=w1[i].astype(BF16), w2=w2[i].astype(BF16),
            ln2_g=ln2_g[i].reshape(1, d), ln2_b=ln2_b[i].reshape(1, d),
        )
        mod = _mod_call(c_all, w_mod, b_mod, i)
        mod_table = mod.reshape(mod_rows, 1, 6 * d)
        mods_l = (mod_table, None)
        mods_c = (mod_table, bsz)

        h_zero = jnp.zeros((bsz, D_STATE, d_inner), F32)
        v_c, gated_c, hc_f, hc_b = _mixer(x_c, mods_c, lw, h_zero, h_zero)
        v_l, gated_l, _, _ = _mixer(x_l, mods_l, lw, hc_f, hc_b)
        x_l = _tail_call(v_l, gated_l, x_l, mods_l, lw, alpha, GRID_W)
        if not last:
            x_c = _tail_call(v_c, gated_c, x_c, mods_c, lw, alpha, ctx.shape[1])
    return x_l
```
